```python
import jax, jax.numpy as jnp
from jax import lax
import numpy as np

D_MODEL = 2048
BATCH = 4
SEQ = 4096
DEPTH = 1

N_META = 16
BLOCK = 128
HEAD_DIM = 64
ATTN_HEADS = D_MODEL // 128
ATTN_WIDTH = ATTN_HEADS * HEAD_DIM
SSM_HEAD_DIM = 64
SSM_HEADS = D_MODEL // 64
SSM_WIDTH = SSM_HEADS * SSM_HEAD_DIM
SSM_GROUPS = 4
SSM_STATE = 128
CONV_WIDTH = 4
CONV_CH = SSM_WIDTH + 2 * SSM_GROUPS * SSM_STATE
MIX_WIDTH = ATTN_WIDTH + SSM_WIDTH
IN_COLS = 3 * ATTN_WIDTH + ATTN_HEADS + SSM_WIDTH + CONV_CH + SSM_HEADS
N_EXPERTS = 32
TOP_K = 4
D_FF = D_MODEL
SWIGLU_LIMIT = 7.0
SWIGLU_ALPHA = 1.702
EPS = 1e-5

kernel_name = "hymba_fox_ssd_moe_block"


def rms_norm(x, g):
    xf = x.astype(jnp.float32)
    y = xf * lax.rsqrt(jnp.mean(xf * xf, axis=-1, keepdims=True) + EPS)
    return (y * g.astype(jnp.float32)).astype(x.dtype)


def forgetting_attention(q, k, v, log_f):
    L = q.shape[1]
    c = jnp.cumsum(log_f, axis=1).transpose(0, 2, 1)
    scale = HEAD_DIM ** -0.5
    bounds = [(0, N_META)] + [(N_META + i * BLOCK, N_META + (i + 1) * BLOCK)
                              for i in range((L - N_META) // BLOCK)]
    outs = []
    for q0, q1 in bounds:
        s = jnp.einsum('bqhd,bkhd->bhqk', q[:, q0:q1], k[:, :q1]).astype(jnp.float32) * scale
        bias = c[:, :, q0:q1, None] - c[:, :, None, :q1]
        causal = jnp.arange(q0, q1)[:, None] >= jnp.arange(q1)[None, :]
        p = jax.nn.softmax(jnp.where(causal, s + bias, -jnp.inf), axis=-1)
        outs.append(jnp.einsum('bhqk,bkhd->bqhd', p.astype(v.dtype), v[:, :q1]))
    return jnp.concatenate(outs, axis=1)


def causal_depthwise_conv(x, w, b):
    K = w.shape[0]
    L = x.shape[1]
    xp = jnp.pad(x, ((0, 0), (K - 1, 0), (0, 0)))
    out = w[0] * xp[:, 0:L]
    for kk in range(1, K):
        out = out + w[kk] * xp[:, kk:kk + L]
    return out + b


def ssd_chunked(xs, a, Bm, Cm):
    b, Lp, H, P = xs.shape
    G, N = Bm.shape[2], Bm.shape[3]
    nc, hg = Lp // BLOCK, H // G
    xs = xs.reshape(b, nc, BLOCK, G, hg, P)
    a = a.reshape(b, nc, BLOCK, G, hg).transpose(0, 3, 4, 1, 2)
    Bm = Bm.reshape(b, nc, BLOCK, G, N)
    Cm = Cm.reshape(b, nc, BLOCK, G, N)
    a_cs = jnp.cumsum(a, axis=-1)
    li = jnp.arange(BLOCK)
    tri = li[:, None] >= li[None, :]
    seg = a_cs[..., :, None] - a_cs[..., None, :]
    decay = jnp.exp(jnp.where(tri, seg, -jnp.inf))
    cb = jnp.einsum('bclgn,bcsgn->bgcls', Cm, Bm)
    m = cb[:, :, None] * decay
    y_diag = jnp.einsum('bghcls,bcsghp->bclghp', m, xs)
    decay_states = jnp.exp(a_cs[..., -1:] - a_cs)
    states = jnp.einsum('bclgn,bghcl,bclghp->bcghpn', Bm, decay_states, xs)
    chunk_sum = a_cs[..., -1]
    incl = jnp.cumsum(chunk_sum, axis=-1)
    excl = incl - chunk_sum
    zi = jnp.arange(nc)
    before = zi[:, None] > zi[None, :]
    dec = jnp.exp(jnp.where(before, excl[..., :, None] - incl[..., None, :], -jnp.inf))
    prev_states = jnp.einsum('bghzc,bcghpn->bzghpn', dec, states)
    y_off = jnp.einsum('bclgn,bcghpn,bghcl->bclghp', Cm, prev_states, jnp.exp(a_cs))
    return (y_diag + y_off).reshape(b, Lp, H, P)


def ssd_mixer(z, xbc, dt_raw, conv_w, conv_b, dt_bias, a_log, d_skip, norm_g):
    b, L, _ = z.shape
    xbc = jax.nn.silu(causal_depthwise_conv(xbc, conv_w, conv_b))
    xs, Bm, Cm = jnp.split(xbc, [SSM_WIDTH, SSM_WIDTH + SSM_GROUPS * SSM_STATE], axis=-1)
    xs = xs.reshape(b, L, SSM_HEADS, SSM_HEAD_DIM)
    Bm = Bm.reshape(b, L, SSM_GROUPS, SSM_STATE).astype(jnp.float32)
    Cm = Cm.reshape(b, L, SSM_GROUPS, SSM_STATE).astype(jnp.float32)
    dt = jax.nn.softplus(dt_raw.astype(jnp.float32) + dt_bias.astype(jnp.float32))
    A = -jnp.exp(a_log.astype(jnp.float32))
    pad = BLOCK - N_META
    def lpad(t):
        return jnp.pad(t, ((0, 0), (pad, 0)) + ((0, 0),) * (t.ndim - 2))
    y = ssd_chunked(lpad(xs.astype(jnp.float32) * dt[..., None]), lpad(dt * A), lpad(Bm), lpad(Cm))[:, pad:]
    y = y + d_skip.astype(jnp.float32)[:, None] * xs.astype(jnp.float32)
    y = y.reshape(b, L, SSM_WIDTH)
    y = y * jax.nn.silu(z.astype(jnp.float32))
    return rms_norm(y, norm_g).astype(z.dtype)


def moe_ffn(x, w_router, b_router, w_gate_up, b_gate_up, w_down, b_down):
    b, L, D = x.shape
    t = x.reshape(b * L, D)
    logits = (t @ w_router + b_router).astype(jnp.float32)
    top_v, top_i = lax.top_k(logits, TOP_K)
    top_w = jax.nn.softmax(top_v, axis=-1)
    gates = jnp.sum(jax.nn.one_hot(top_i, N_EXPERTS, dtype=jnp.float32) * top_w[..., None], axis=1)
    y = jnp.zeros_like(t)
    for e in range(N_EXPERTS):
        gu = t @ w_gate_up[e] + b_gate_up[e]
        g = jnp.minimum(gu[:, :D_FF], SWIGLU_LIMIT)
        u = jnp.clip(gu[:, D_FF:], -SWIGLU_LIMIT, SWIGLU_LIMIT)
        hdn = g * jax.nn.sigmoid(SWIGLU_ALPHA * g) * (u + 1.0)
        y = y + gates[:, e:e + 1].astype(t.dtype) * (hdn @ w_down[e] + b_down[e])
    return y.reshape(b, L, D)


def setup_inputs(seed: int = 0) -> dict:
    key = jax.random.key(seed)
    ks = jax.random.split(key, 24)
    f32 = jnp.float32
    def nrm(k, shape, scale):
        return jax.random.normal(k, shape, f32) * scale
    dt0 = jnp.exp(jax.random.uniform(ks[9], (DEPTH, SSM_HEADS), f32, np.log(1e-3), np.log(1e-1)))
    return {
        "x": nrm(ks[0], (BATCH, SEQ, D_MODEL), 1.0),
        "meta_tokens": nrm(ks[1], (N_META, D_MODEL), 1.0),
        "norm_mix_g": 1.0 + nrm(ks[2], (DEPTH, D_MODEL), 0.02),
        "w_in": nrm(ks[3], (DEPTH, D_MODEL, IN_COLS), D_MODEL ** -0.5),
        "fgate_b": 3.0 + nrm(ks[4], (DEPTH, ATTN_HEADS), 0.5),
        "attn_norm_g": 1.0 + nrm(ks[5], (DEPTH, ATTN_WIDTH), 0.02),
        "conv_w": nrm(ks[6], (DEPTH, CONV_WIDTH, CONV_CH), CONV_WIDTH ** -0.5),
        "conv_b": nrm(ks[7], (DEPTH, CONV_CH), 0.02),
        "dt_bias": dt0 + jnp.log(-jnp.expm1(-dt0)),
        "a_log": jnp.log(jax.random.uniform(ks[10], (DEPTH, SSM_HEADS), f32, 1.0, 16.0)),
        "d_skip": 1.0 + nrm(ks[11], (DEPTH, SSM_HEADS), 0.1),
        "ssm_norm_g": 1.0 + nrm(ks[12], (DEPTH, SSM_WIDTH), 0.02),
        "w_out": nrm(ks[13], (DEPTH, MIX_WIDTH, D_MODEL), MIX_WIDTH ** -0.5),
        "norm_ffn_g": 1.0 + nrm(ks[14], (DEPTH, D_MODEL), 0.02),
        "w_router": nrm(ks[15], (DEPTH, D_MODEL, N_EXPERTS), D_MODEL ** -0.5),
        "b_router": nrm(ks[16], (DEPTH, N_EXPERTS), 0.01),
        "w_gate_up": nrm(ks[17], (DEPTH, N_EXPERTS, D_MODEL, 2 * D_FF), D_MODEL ** -0.5),
        "b_gate_up": nrm(ks[18], (DEPTH, N_EXPERTS, 2 * D_FF), 0.01),
        "w_down": nrm(ks[19], (DEPTH, N_EXPERTS, D_FF, D_MODEL), D_FF ** -0.5),
        "b_down": nrm(ks[20], (DEPTH, N_EXPERTS, D_MODEL), 0.01),
        "final_norm_g": 1.0 + nrm(ks[21], (D_MODEL,), 0.02),
    }


def reference(x, meta_tokens, norm_mix_g, w_in, fgate_b, attn_norm_g, conv_w, conv_b,
              dt_bias, a_log, d_skip, ssm_norm_g, w_out, norm_ffn_g, w_router, b_router,
              w_gate_up, b_gate_up, w_down, b_down, final_norm_g):
    b = x.shape[0]
    h = jnp.concatenate([jnp.broadcast_to(meta_tokens[None].astype(x.dtype), (b, N_META, D_MODEL)), x], axis=1)
    L = h.shape[1]
    sizes = [ATTN_WIDTH, ATTN_WIDTH, ATTN_WIDTH, ATTN_HEADS, SSM_WIDTH, CONV_CH, SSM_HEADS]
    offs = []
    run = 0
    for s_ in sizes[:-1]:
        run += s_
        offs.append(run)
    for l in range(DEPTH):
        n = rms_norm(h, norm_mix_g[l])
        proj = n @ w_in[l]
        q, k, v, f_logit, z, xbc, dt_raw = jnp.split(proj, offs, axis=-1)
        q = q.reshape(b, L, ATTN_HEADS, HEAD_DIM)
        k = k.reshape(b, L, ATTN_HEADS, HEAD_DIM)
        v = v.reshape(b, L, ATTN_HEADS, HEAD_DIM)
        log_f = jax.nn.log_sigmoid(f_logit.astype(jnp.float32) + fgate_b[l].astype(jnp.float32))
        attn = forgetting_attention(q, k, v, log_f).reshape(b, L, ATTN_WIDTH)
        attn = rms_norm(attn, attn_norm_g[l])
        ssm = ssd_mixer(z, xbc, dt_raw, conv_w[l], conv_b[l], dt_bias[l], a_log[l], d_skip[l], ssm_norm_g[l])
        mixed = jnp.concatenate([attn, ssm.astype(attn.dtype)], axis=-1) @ w_out[l]
        h = h + mixed.astype(h.dtype)
        n2 = rms_norm(h, norm_ffn_g[l])
        h = h + moe_ffn(n2, w_router[l], b_router[l], w_gate_up[l], b_gate_up[l], w_down[l], b_down[l]).astype(h.dtype)
    return rms_norm(h[:, N_META:], final_norm_g)
```

```python
import functools

import jax
import jax.numpy as jnp
from jax import lax
from jax.experimental import pallas as pl
from jax.experimental.pallas import tpu as pltpu

F32 = jnp.float32
BF16 = jnp.bfloat16
I32 = jnp.int32

N_META = 16
CHUNK = 128
HEAD_DIM = 64
SSM_GROUPS = 4
SSM_STATE = 128
CONV_WIDTH = 4
N_EXPERTS = 32
TOP_K = 4
SWIGLU_LIMIT = 7.0
SWIGLU_ALPHA = 1.702
EPS = 1e-5
LANES = 128
HALO = 8
NEG = -1e30
VMEM_LIMIT_BYTES = 56 * 1024 * 1024


def _cparams(*sem):
    return pltpu.CompilerParams(dimension_semantics=sem, vmem_limit_bytes=VMEM_LIMIT_BYTES)


def _sigmoid(x):
    return 1.0 / (1.0 + jnp.exp(-x))


def _softplus(x):
    return jnp.maximum(x, 0.0) + jnp.log(1.0 + jnp.exp(-jnp.abs(x)))


def _split_bf16(x):
    hi = x.astype(BF16)
    lo = (x - hi.astype(F32)).astype(BF16)
    return hi, lo


def _rmsnorm_kernel(x_ref, g_ref, o_ref):
    x = x_ref[...]
    ms = jnp.mean(x * x, axis=-1, keepdims=True)
    o_ref[...] = (x * lax.rsqrt(ms + EPS) * g_ref[...]).astype(o_ref.dtype)


def _rmsnorm_rows(x, g, tm):
    m, d = x.shape
    tm = min(tm, m)
    return pl.pallas_call(
        _rmsnorm_kernel,
        grid=(m // tm,),
        in_specs=[pl.BlockSpec((tm, d), lambda i: (i, 0)), pl.BlockSpec((1, d), lambda i: (0, 0))],
        out_specs=pl.BlockSpec((tm, d), lambda i: (i, 0)),
        out_shape=jax.ShapeDtypeStruct((m, d), BF16),
        compiler_params=_cparams("parallel"),
        name="rmsnorm_rows",
    )(x, g.reshape(1, d))


def _matmul_kernel(a_ref, w_ref, o_ref):
    o_ref[...] = jnp.dot(a_ref[...], w_ref[...], preferred_element_type=F32).astype(o_ref.dtype)


def _pick_tile(n, pref, unit=LANES):
    best = unit
    for t in range(unit, min(pref, n) + 1, unit):
        if n % t == 0:
            best = t
    return best


def _matmul(a, w, out_dtype, tm, tn, name):
    m, k = a.shape
    n = w.shape[1]
    tm, tn = min(tm, m), _pick_tile(n, tn)
    return pl.pallas_call(
        _matmul_kernel,
        grid=(n // tn, m // tm),
        in_specs=[pl.BlockSpec((tm, k), lambda j, i: (i, 0)), pl.BlockSpec((k, tn), lambda j, i: (0, j))],
        out_specs=pl.BlockSpec((tm, tn), lambda j, i: (i, j)),
        out_shape=jax.ShapeDtypeStruct((m, n), out_dtype),
        compiler_params=_cparams("parallel", "parallel"),
        name=name,
    )(a, w)


def _fgate_cumsum_kernel(f_ref, b_ref, init_ref, o_ref, carry_ref, *, blk, n_heads, n_lead_pad):
    j = pl.program_id(1)

    @pl.when(j == 0)
    def _():
        carry_ref[...] = init_ref[...]

    x = f_ref[...] + b_ref[...]
    lf = jnp.minimum(x, 0.0) - jnp.log(1.0 + jnp.exp(-jnp.abs(x)))
    if n_lead_pad:
        row = lax.broadcasted_iota(I32, lf.shape, 0) + j * blk
        lf = jnp.where(row < n_lead_pad, 0.0, lf)
    lft = lf.T
    hi, lo = _split_bf16(lft)
    upper = (lax.broadcasted_iota(I32, (blk, blk), 0) <= lax.broadcasted_iota(I32, (blk, blk), 1)).astype(BF16)
    c = (jnp.dot(hi, upper, preferred_element_type=F32) + jnp.dot(lo, upper, preferred_element_type=F32)
         + carry_ref[...])
    o_ref[0] = c[:n_heads, :]
    carry_ref[...] = c[:, blk - 1:blk]


def _fgate_cumsum(f_small, bias_row, init_col, n_batch, n_heads, n_lead_pad=0, blk=512):
    rows = f_small.shape[0]
    s = rows // n_batch
    blk = min(blk, s)
    nb = s // blk
    return pl.pallas_call(
        functools.partial(_fgate_cumsum_kernel, blk=blk, n_heads=n_heads, n_lead_pad=n_lead_pad),
        grid=(n_batch, nb),
        in_specs=[pl.BlockSpec((blk, LANES), lambda b, j: (b * nb + j, 0)),
                  pl.BlockSpec((1, LANES), lambda b, j: (0, 0)),
                  pl.BlockSpec((LANES, 1), lambda b, j: (0, 0))],
        out_specs=pl.BlockSpec((1, n_heads, blk), lambda b, j: (b, 0, j)),
        out_shape=jax.ShapeDtypeStruct((n_batch, n_heads, s), F32),
        scratch_shapes=[pltpu.VMEM((LANES, 1), F32)],
        compiler_params=_cparams("parallel", "arbitrary"),
        name="fgate_cumsum",
    )(f_small, bias_row, init_col)


def _attn_kernel(q_ref, k_ref, v_ref, km_ref, vm_ref, ct_ref, ctm_ref, o_ref, m_ref, l_ref, acc_ref,
                 *, tq, n_q, n_meta_pad):
    lane = lax.broadcasted_iota(I32, (tq, LANES), 1)
    lo_half = lane < HEAD_DIM
    causal = lax.broadcasted_iota(I32, (tq, tq), 1) <= lax.broadcasted_iota(I32, (tq, tq), 0)
    meta_valid = lax.broadcasted_iota(I32, (1, LANES), 1) >= n_meta_pad
    scale = HEAD_DIM ** -0.5
    nt = (((1,), (1,)), ((), ()))

    def q_block(qi, carry):
        q0 = pl.multiple_of(qi * tq, tq)
        q = q_ref[pl.ds(q0, tq), :] * jnp.asarray(scale, BF16)
        zero = jnp.zeros_like(q)
        q_heads = (jnp.where(lo_half, q, zero), jnp.where(lo_half, zero, q))

        for h in range(2):
            cref = ct_ref[0, 0, h:h + 1, pl.ds(q0, LANES)][:, 0:1]

            def scores(kblk, ck, mask):
                s = lax.dot_general(q_heads[h], kblk, nt, preferred_element_type=F32)
                t = s - (ck - cref)
                return t if mask is None else jnp.where(mask, t, NEG)

            t = scores(km_ref[...], ctm_ref[0, 0, h:h + 1, :], meta_valid)
            m0 = jnp.max(t, axis=-1, keepdims=True)
            p = jnp.exp(t - m0)
            m_ref[h] = m0
            l_ref[h] = jnp.sum(p, axis=-1, keepdims=True)
            acc_ref[h] = jnp.dot(p.astype(BF16), vm_ref[...], preferred_element_type=F32)

            def update(k0, mask):
                t = scores(k_ref[pl.ds(k0, tq), :], ct_ref[0, 0, h:h + 1, pl.ds(k0, tq)], mask)
                m_prev = m_ref[h]
                m_new = jnp.maximum(m_prev, jnp.max(t, axis=-1, keepdims=True))
                alpha = jnp.exp(m_prev - m_new)
                p = jnp.exp(t - m_new)
                l_ref[h] = alpha * l_ref[h] + jnp.sum(p, axis=-1, keepdims=True)
                acc_ref[h] = alpha * acc_ref[h] + jnp.dot(p.astype(BF16), v_ref[pl.ds(k0, tq), :],
                                                          preferred_element_type=F32)
                m_ref[h] = m_new

            def k_block(j, c):
                update(pl.multiple_of(j * tq, tq), None)
                return c

            lax.fori_loop(0, qi, k_block, 0)
            update(q0, causal)

        out = jnp.where(lo_half, acc_ref[0] / l_ref[0], acc_ref[1] / l_ref[1])
        o_ref[pl.ds(q0, tq), :] = out.astype(o_ref.dtype)
        return carry

    lax.fori_loop(0, n_q, q_block, 0)


def _attention(big, big_meta, ct, ct_meta, n_batch, seq, n_pairs, col_q, col_k, col_v, tq=256):
    tq = min(tq, seq)
    n_q = seq // tq
    mrows = big_meta.shape[0]
    kern = functools.partial(_attn_kernel, tq=tq, n_q=n_q, n_meta_pad=mrows - N_META)
    blk = lambda c0: pl.BlockSpec((seq, LANES), lambda b, p, c0=c0: (b, c0 + p))
    mblk = lambda c0: pl.BlockSpec((mrows, LANES), lambda b, p, c0=c0: (0, c0 + p))
    return pl.pallas_call(
        kern,
        grid=(n_batch, n_pairs),
        in_specs=[blk(col_q), blk(col_k), blk(col_v), mblk(col_k), mblk(col_v),
                  pl.BlockSpec((1, 1, 2, seq), lambda b, p: (b, p, 0, 0)),
                  pl.BlockSpec((1, 1, 2, mrows), lambda b, p: (0, p, 0, 0))],
        out_specs=pl.BlockSpec((seq, LANES), lambda b, p: (b, p)),
        out_shape=jax.ShapeDtypeStruct((n_batch * seq, n_pairs * LANES), BF16),
        scratch_shapes=[pltpu.VMEM((2, tq, 1), F32), pltpu.VMEM((2, tq, 1), F32),
                        pltpu.VMEM((2, tq, LANES), F32)],
        compiler_params=_cparams("parallel", "parallel"),
        name="fox_attention",
    )(big, big, big, big_meta, big_meta, ct, ct_meta)


def _ssd_kernel(x_ref, b_ref, c_ref, z_ref, dt_ref, wx_ref, wb_ref, wc_ref, bx_ref, bb_ref, bc_ref,
                hp_ref, dsk_ref, hx0_ref, hb0_ref, hc0_ref, s0_ref,
                y_ref, sout_ref, hxo_ref, hbo_ref, hco_ref,
                xpad_ref, bpad_ref, cpad_ref, state_ref, *, n_pairs, n_lead_pad):
    c = pl.program_id(2)
    n_c = pl.num_programs(2)
    L = CHUNK

    @pl.when(c == 0)
    def _():
        xpad_ref[0:HALO, :] = hx0_ref[0].astype(F32)
        bpad_ref[0:HALO, :] = hb0_ref[0].astype(F32)
        cpad_ref[0:HALO, :] = hc0_ref[0].astype(F32)
        state_ref[...] = s0_ref[0, 0]

    def conv_silu(pad_ref, raw_ref, w_ref, bias_ref):
        pad_ref[HALO:HALO + L, :] = raw_ref[...].astype(F32)
        acc = bias_ref[...]
        for kk in range(CONV_WIDTH):
            off = HALO - (CONV_WIDTH - 1) + kk
            acc = acc + w_ref[kk:kk + 1, :] * pad_ref[off:off + L, :]
        return acc * _sigmoid(acc)

    xs = conv_silu(xpad_ref, x_ref, wx_ref, bx_ref)
    bm = conv_silu(bpad_ref, b_ref, wb_ref, bb_ref)
    cm = conv_silu(cpad_ref, c_ref, wc_ref, bc_ref)

    hp = hp_ref[0]
    dt = _softplus(dt_ref[...] + hp[0:1, :])
    if n_lead_pad:
        row = lax.broadcasted_iota(I32, dt.shape, 0) + c * L
        dt = jnp.where(row < n_lead_pad, 0.0, dt)
    a = dt * (-jnp.exp(hp[1:2, :]))

    ri = lax.broadcasted_iota(I32, (L, L), 0)
    ci = lax.broadcasted_iota(I32, (L, L), 1)
    tri = ri >= ci
    tri_bf = tri.astype(BF16)
    a_hi, a_lo = _split_bf16(a)
    a_cs = jnp.dot(tri_bf, a_hi, preferred_element_type=F32) + jnp.dot(tri_bf, a_lo, preferred_element_type=F32)
    a_cst = a_cs.T
    a_last = a_cs[L - 1:L, :]
    ea = jnp.exp(a_cs)
    wdec = jnp.exp(a_last - a_cs)
    etot = jnp.exp(a_last)

    cm_bf = cm.astype(BF16)
    bm_bf = bm.astype(BF16)
    cb = lax.dot_general(cm_bf, bm_bf, (((1,), (1,)), ((), ())), preferred_element_type=F32)
    y_off = jnp.dot(cm_bf, state_ref[...].astype(BF16), preferred_element_type=F32)

    lane = lax.broadcasted_iota(I32, (L, LANES), 1)
    lo_half = lane < HEAD_DIM
    lo_row = lo_half[0:1, :]
    xw_parts = []
    etot_parts = []
    for p in range(n_pairs):
        ha, hb = 2 * p, 2 * p + 1
        sl = slice(p * LANES, (p + 1) * LANES)
        pick = lambda arr: jnp.where(lo_half, arr[:, ha:ha + 1], arr[:, hb:hb + 1])
        xs_p = xs[:, sl]
        xdt = xs_p * pick(dt)
        xdt_bf = xdt.astype(BF16)
        y_heads = []
        for hh in (ha, hb):
            seg = a_cs[:, hh:hh + 1] - a_cst[hh:hh + 1, :]
            dec = jnp.where(tri, jnp.exp(seg), 0.0)
            y_heads.append(jnp.dot((cb * dec).astype(BF16), xdt_bf, preferred_element_type=F32))
        y = jnp.where(lo_half, y_heads[0], y_heads[1]) + y_off[:, sl] * pick(ea) + dsk_ref[:, sl] * xs_p
        zf = z_ref[:, sl].astype(F32)
        y_ref[:, sl] = (y * (zf * _sigmoid(zf))).astype(y_ref.dtype)
        xw_parts.append((xdt * pick(wdec)).astype(BF16))
        etot_parts.append(jnp.where(lo_row, etot[:, ha:ha + 1], etot[:, hb:hb + 1]))

    xw = xw_parts[0] if n_pairs == 1 else jnp.concatenate(xw_parts, axis=1)
    etot_x = etot_parts[0] if n_pairs == 1 else jnp.concatenate(etot_parts, axis=1)
    state_ref[...] = state_ref[...] * etot_x + jnp.dot(bm.T.astype(BF16), xw, preferred_element_type=F32)

    xpad_ref[0:HALO, :] = xpad_ref[L:L + HALO, :]
    bpad_ref[0:HALO, :] = bpad_ref[L:L + HALO, :]
    cpad_ref[0:HALO, :] = cpad_ref[L:L + HALO, :]

    @pl.when(c == n_c - 1)
    def _():
        sout_ref[0, 0] = state_ref[...]
        hxo_ref[0] = xpad_ref[0:HALO, :]
        hbo_ref[0] = bpad_ref[0:HALO, :]
        hco_ref[0] = cpad_ref[0:HALO, :]


def _ssd(big, dtg, conv_w, conv_b, head_params, dskip_x, halo_x, halo_b, halo_c, state0,
         n_batch, seq, d_inner, col_z, col_x, n_lead_pad=0):
    G, N, L = SSM_GROUPS, SSM_STATE, CHUNK
    pg = d_inner // G
    n_pairs = pg // LANES
    n_c = seq // L
    cx, cz = col_x // pg, col_z // pg
    cb0 = (col_x + d_inner) // N
    cc0 = cb0 + G
    row = lambda b, g, c: b * n_c + c
    kern = functools.partial(_ssd_kernel, n_pairs=n_pairs, n_lead_pad=n_lead_pad)
    in_specs = [
        pl.BlockSpec((L, pg), lambda b, g, c: (row(b, g, c), cx + g)),
        pl.BlockSpec((L, N), lambda b, g, c: (row(b, g, c), cb0 + g)),
        pl.BlockSpec((L, N), lambda b, g, c: (row(b, g, c), cc0 + g)),
        pl.BlockSpec((L, pg), lambda b, g, c: (row(b, g, c), cz + g)),
        pl.BlockSpec((L, LANES), lambda b, g, c: (row(b, g, c), g)),
        pl.BlockSpec((CONV_WIDTH, pg), lambda b, g, c: (0, g)),
        pl.BlockSpec((CONV_WIDTH, N), lambda b, g, c: (0, d_inner // N + g)),
        pl.BlockSpec((CONV_WIDTH, N), lambda b, g, c: (0, d_inner // N + G + g)),
        pl.BlockSpec((1, pg), lambda b, g, c: (0, g)),
        pl.BlockSpec((1, N), lambda b, g, c: (0, d_inner // N + g)),
        pl.BlockSpec((1, N), lambda b, g, c: (0, d_inner // N + G + g)),
        pl.BlockSpec((1, 8, LANES), lambda b, g, c: (g, 0, 0)),
        pl.BlockSpec((1, pg), lambda b, g, c: (0, g)),
        pl.BlockSpec((1, HALO, pg), lambda b, g, c: (0, 0, g)),
        pl.BlockSpec((1, HALO, N), lambda b, g, c: (0, 0, g)),
        pl.BlockSpec((1, HALO, N), lambda b, g, c: (0, 0, g)),
        pl.BlockSpec((1, 1, N, pg), lambda b, g, c: (0, g, 0, 0)),
    ]
    out_specs = [
        pl.BlockSpec((L, pg), lambda b, g, c: (row(b, g, c), g)),
        pl.BlockSpec((1, 1, N, pg), lambda b, g, c: (b, g, 0, 0)),
        pl.BlockSpec((1, HALO, pg), lambda b, g, c: (b, 0, g)),
        pl.BlockSpec((1, HALO, N), lambda b, g, c: (b, 0, g)),
        pl.BlockSpec((1, HALO, N), lambda b, g, c: (b, 0, g)),
    ]
    out_shape = [
        jax.ShapeDtypeStruct((n_batch * seq, d_inner), F32),
        jax.ShapeDtypeStruct((n_batch, G, N, pg), F32),
        jax.ShapeDtypeStruct((n_batch, HALO, d_inner), F32),
        jax.ShapeDtypeStruct((n_batch, HALO, G * N), F32),
        jax.ShapeDtypeStruct((n_batch, HALO, G * N), F32),
    ]
    return pl.pallas_call(
        kern,
        grid=(n_batch, G, n_c),
        in_specs=in_specs,
        out_specs=out_specs,
        out_shape=out_shape,
        scratch_shapes=[pltpu.VMEM((L + HALO, pg), F32), pltpu.VMEM((L + HALO, N), F32),
                        pltpu.VMEM((L + HALO, N), F32), pltpu.VMEM((N, pg), F32)],
        compiler_params=_cparams("parallel", "parallel", "arbitrary"),
        name="ssd_scan",
    )(big, big, big, big, dtg, conv_w, conv_w, conv_w, conv_b, conv_b, conv_b,
      head_params, dskip_x, halo_x, halo_b, halo_c, state0)


def _outproj_kernel(attn_ref, y_ref, x_ref, ga_ref, gs_ref, w_ref, gf_ref, wr_ref, br_ref,
                    h1_ref, n2_ref, logit_ref, *, attn_width):
    def norm(v, g):
        return v * lax.rsqrt(jnp.mean(v * v, axis=-1, keepdims=True) + EPS) * g

    an = norm(attn_ref[...].astype(F32), ga_ref[...]).astype(BF16)
    yn = norm(y_ref[...], gs_ref[...]).astype(BF16)
    mixed = (jnp.dot(an, w_ref[0:attn_width, :], preferred_element_type=F32)
             + jnp.dot(yn, w_ref[attn_width:, :], preferred_element_type=F32))
    h1 = x_ref[...] + mixed
    n2 = norm(h1, gf_ref[...])
    h1_ref[...] = h1
    n2_ref[...] = n2
    logit_ref[...] = jnp.dot(n2.astype(BF16), wr_ref[...], preferred_element_type=F32) + br_ref[...]


def _outproj(attn, yssm, x, g_attn, g_ssm, w_out, g_ffn, w_router, b_router, tm=512):
    m, d = x.shape
    aw, sw = attn.shape[1], yssm.shape[1]
    tm = min(tm, m)
    row = lambda w: pl.BlockSpec((tm, w), lambda i: (i, 0))
    full = lambda r, c: pl.BlockSpec((r, c), lambda i: (0, 0))
    return pl.pallas_call(
        functools.partial(_outproj_kernel, attn_width=aw),
        grid=(m // tm,),
        in_specs=[row(aw), row(sw), row(d), full(1, aw), full(1, sw), full(aw + sw, d), full(1, d),
                  full(d, LANES), full(1, LANES)],
        out_specs=[row(d), row(d), row(LANES)],
        out_shape=[jax.ShapeDtypeStruct((m, d), F32), jax.ShapeDtypeStruct((m, d), F32),
                   jax.ShapeDtypeStruct((m, LANES), F32)],
        compiler_params=_cparams("parallel"),
        name="outproj_router",
    )(attn, yssm, x, g_attn.reshape(1, aw), g_ssm.reshape(1, sw), w_out, g_ffn.reshape(1, d),
      w_router, b_router)


def _route_kernel(logit_ref, idx_ref, gate_ref, cnt_ref, carry_ref, *, tm):
    i = pl.program_id(0)

    @pl.when(i == 0)
    def _():
        carry_ref[...] = jnp.zeros_like(carry_ref)

    work = logit_ref[...]
    lane = lax.broadcasted_iota(I32, work.shape, 1)
    lane_f = lane.astype(F32)
    vals, ids, hots = [], [], []
    for _ in range(TOP_K):
        mk = jnp.max(work, axis=-1, keepdims=True)
        ik = jnp.min(jnp.where(work == mk, lane_f, float(LANES)), axis=-1, keepdims=True)
        hot = lane_f == ik
        work = jnp.where(hot, -jnp.inf, work)
        vals.append(mk)
        ids.append(ik)
        hots.append(hot)
    exps = [jnp.exp(v - vals[0]) for v in vals]
    denom = exps[0] + exps[1] + exps[2] + exps[3]
    sel = jnp.zeros(work.shape, F32)
    for hot in hots:
        sel = sel + hot.astype(F32)
    strict = (lax.broadcasted_iota(I32, (tm, tm), 1) < lax.broadcasted_iota(I32, (tm, tm), 0)).astype(BF16)
    before = jnp.dot(strict, sel.astype(BF16), preferred_element_type=F32) + carry_ref[...]
    idx_out = jnp.zeros(work.shape, I32)
    gate_out = jnp.zeros(work.shape, F32)
    for k in range(TOP_K):
        rank = jnp.sum(jnp.where(hots[k], before, 0.0), axis=-1, keepdims=True).astype(I32)
        idx_out = jnp.where(lane == k, ids[k].astype(I32), idx_out)
        idx_out = jnp.where(lane == TOP_K + k, rank, idx_out)
        gate_out = jnp.where(lane == k, exps[k] / denom, gate_out)
    idx_ref[...] = idx_out
    gate_ref[...] = gate_out
    carry_ref[...] = carry_ref[...] + jnp.sum(sel, axis=0, keepdims=True)
    cnt_ref[...] = carry_ref[...]


def _route(logits, tm=512):
    m = logits.shape[0]
    tm = min(tm, m)
    return pl.pallas_call(
        functools.partial(_route_kernel, tm=tm),
        grid=(m // tm,),
        in_specs=[pl.BlockSpec((tm, LANES), lambda i: (i, 0))],
        out_specs=[pl.BlockSpec((tm, LANES), lambda i: (i, 0)), pl.BlockSpec((tm, LANES), lambda i: (i, 0)),
                   pl.BlockSpec((1, LANES), lambda i: (0, 0))],
        out_shape=[jax.ShapeDtypeStruct((m, LANES), I32), jax.ShapeDtypeStruct((m, LANES), F32),
                   jax.ShapeDtypeStruct((1, LANES), F32)],
        scratch_shapes=[pltpu.VMEM((1, LANES), F32)],
        compiler_params=_cparams("arbitrary"),
        name="route_top4",
    )(logits)


def _dispatch_kernel(dest_ref, src_ref, init_ref, out_ref, sem, *, td):
    del init_ref
    base = pl.program_id(0) * td

    def copy(t, d):
        return pltpu.make_async_copy(src_ref.at[pl.ds(t, 1)], out_ref.at[pl.ds(d, 1)], sem)

    def issue(j, c):
        for k in range(TOP_K):
            copy(base + j, dest_ref[j * TOP_K + k]).start()
        return c

    lax.fori_loop(0, td, issue, 0)

    def drain(j, c):
        copy(0, 0).wait()
        return c

    lax.fori_loop(0, td * TOP_K, drain, 0)


def _dispatch(dest_flat, src, n_rows_out, td=256):
    m, d = src.shape
    td = min(td, m)
    init = jnp.zeros((n_rows_out, d), src.dtype)
    return pl.pallas_call(
        functools.partial(_dispatch_kernel, td=td),
        grid=(m // td,),
        in_specs=[pl.BlockSpec((td * TOP_K,), lambda i: (i,), memory_space=pltpu.SMEM),
                  pl.BlockSpec(memory_space=pl.ANY), pl.BlockSpec(memory_space=pl.ANY)],
        out_specs=pl.BlockSpec(memory_space=pl.ANY),
        out_shape=jax.ShapeDtypeStruct((n_rows_out, d), src.dtype),
        scratch_shapes=[pltpu.SemaphoreType.DMA],
        input_output_aliases={2: 0},
        compiler_params=_cparams("arbitrary"),
        name="moe_dispatch",
    )(dest_flat, src, init)


def _gate_up_kernel(te_ref, nu_ref, x_ref, wg_ref, wu_ref, bg_ref, bu_ref, o_ref):
    @pl.when(pl.program_id(1) < nu_ref[0])
    def _():
        x = x_ref[...].astype(BF16)
        g = jnp.dot(x, wg_ref[0], preferred_element_type=F32) + bg_ref[0]
        u = jnp.dot(x, wu_ref[0], preferred_element_type=F32) + bu_ref[0]
        g = jnp.minimum(g, SWIGLU_LIMIT)
        u = jnp.clip(u, -SWIGLU_LIMIT, SWIGLU_LIMIT)
        o_ref[...] = (g * _sigmoid(SWIGLU_ALPHA * g) * (u + 1.0)).astype(o_ref.dtype)

    @pl.when(pl.program_id(1) >= nu_ref[0])
    def _():
        o_ref[...] = jnp.zeros_like(o_ref)


def _grouped_gate_up(tile_expert, n_used, xs, w_gu, b_gu, tm, tn=512):
    mp, d = xs.shape
    ff = w_gu.shape[2] // 2
    tn = min(tn, ff)
    nt = ff // tn
    grid_spec = pltpu.PrefetchScalarGridSpec(
        num_scalar_prefetch=2,
        grid=(nt, mp // tm),
        in_specs=[pl.BlockSpec((tm, d), lambda n, m, te, nu: (m, 0)),
                  pl.BlockSpec((1, d, tn), lambda n, m, te, nu: (te[m], 0, n)),
                  pl.BlockSpec((1, d, tn), lambda n, m, te, nu: (te[m], 0, nt + n)),
                  pl.BlockSpec((1, 1, tn), lambda n, m, te, nu: (te[m], 0, n)),
                  pl.BlockSpec((1, 1, tn), lambda n, m, te, nu: (te[m], 0, nt + n))],
        out_specs=pl.BlockSpec((tm, tn), lambda n, m, te, nu: (m, n)),
    )
    return pl.pallas_call(
        _gate_up_kernel,
        grid_spec=grid_spec,
        out_shape=jax.ShapeDtypeStruct((mp, ff), BF16),
        compiler_params=_cparams("parallel", "arbitrary"),
        name="moe_gate_up",
    )(tile_expert, n_used, xs, w_gu, w_gu, b_gu, b_gu)


def _down_kernel(te_ref, nu_ref, h_ref, w_ref, b_ref, o_ref):
    @pl.when(pl.program_id(1) < nu_ref[0])
    def _():
        o_ref[...] = jnp.dot(h_ref[...], w_ref[0], preferred_element_type=F32) + b_ref[0]

    @pl.when(pl.program_id(1) >= nu_ref[0])
    def _():
        o_ref[...] = jnp.zeros_like(o_ref)


def _grouped_down(tile_expert, n_used, hdn, w_d, b_d, tm, tn=1024):
    mp, ff = hdn.shape
    d = w_d.shape[2]
    tn = min(tn, d)
    grid_spec = pltpu.PrefetchScalarGridSpec(
        num_scalar_prefetch=2,
        grid=(d // tn, mp // tm),
        in_specs=[pl.BlockSpec((tm, ff), lambda n, m, te, nu: (m, 0)),
                  pl.BlockSpec((1, ff, tn), lambda n, m, te, nu: (te[m], 0, n)),
                  pl.BlockSpec((1, 1, tn), lambda n, m, te, nu: (te[m], 0, n))],
        out_specs=pl.BlockSpec((tm, tn), lambda n, m, te, nu: (m, n)),
    )
    return pl.pallas_call(
        _down_kernel,
        grid_spec=grid_spec,
        out_shape=jax.ShapeDtypeStruct((mp, d), F32),
        compiler_params=_cparams("parallel", "arbitrary"),
        name="moe_down",
    )(tile_expert, n_used, hdn, w_d, b_d)


def _combine_kernel(dest_ref, ys_ref, gate_ref, h1_ref, g_ref, o_ref, buf_ref, sem, *, tc):
    def copy(d, k, j):
        return pltpu.make_async_copy(ys_ref.at[pl.ds(d, 1)], buf_ref.at[k, pl.ds(j, 1)], sem)

    def issue(j, c):
        for k in range(TOP_K):
            copy(dest_ref[j * TOP_K + k], k, j).start()
        return c

    lax.fori_loop(0, tc, issue, 0)

    def drain(j, c):
        copy(0, 0, 0).wait()
        return c

    lax.fori_loop(0, tc * TOP_K, drain, 0)

    gates = gate_ref[...]
    h2 = h1_ref[...]
    for k in range(TOP_K):
        h2 = h2 + gates[:, k:k + 1] * buf_ref[k]
    o_ref[...] = h2 * lax.rsqrt(jnp.mean(h2 * h2, axis=-1, keepdims=True) + EPS) * g_ref[...]


def _combine(dest_flat, ys, gates, h1, g_final, tc=256):
    m, d = h1.shape
    tc = min(tc, m)
    return pl.pallas_call(
        functools.partial(_combine_kernel, tc=tc),
        grid=(m // tc,),
        in_specs=[pl.BlockSpec((tc * TOP_K,), lambda i: (i,), memory_space=pltpu.SMEM),
                  pl.BlockSpec(memory_space=pl.ANY),
                  pl.BlockSpec((tc, LANES), lambda i: (i, 0)),
                  pl.BlockSpec((tc, d), lambda i: (i, 0)),
                  pl.BlockSpec((1, d), lambda i: (0, 0))],
        out_specs=pl.BlockSpec((tc, d), lambda i: (i, 0)),
        out_shape=jax.ShapeDtypeStruct((m, d), F32),
        scratch_shapes=[pltpu.VMEM((TOP_K, tc, d), F32), pltpu.SemaphoreType.DMA],
        compiler_params=_cparams("arbitrary"),
        name="moe_combine",
    )(dest_flat, ys, gates, h1, g_final.reshape(1, d))


def _lane_pad(v, width=LANES):
    return jnp.pad(v, ((0, 0), (0, width - v.shape[1])))


def kernel(x, meta_tokens, norm_mix_g, w_in, fgate_b, attn_norm_g, conv_w, conv_b, dt_bias, a_log, d_skip,
           ssm_norm_g, w_out, norm_ffn_g, w_router, b_router, w_gate_up, b_gate_up, w_down, b_down,
           final_norm_g):
    n_batch, seq, d = x.shape
    aw = (d // 128) * HEAD_DIM
    n_ah = aw // HEAD_DIM
    sw = d
    n_sh = sw // HEAD_DIM
    hpg = n_sh // SSM_GROUPS
    gn = SSM_GROUPS * SSM_STATE
    tokens = n_batch * seq
    lyr = 0

    o_f = 3 * aw
    o_z = o_f + n_ah
    o_xbc = o_z + sw
    o_dt = o_xbc + sw + 2 * gn
    wi = w_in[lyr]
    w_big = jnp.concatenate([wi[:, :o_f], wi[:, o_z:o_dt]], axis=1).astype(BF16)
    w_dt = wi[:, o_dt:o_dt + n_sh].reshape(d, SSM_GROUPS, hpg)
    w_dt = jnp.pad(w_dt, ((0, 0), (0, 0), (0, LANES - hpg))).reshape(d, SSM_GROUPS * LANES)
    w_small = jnp.concatenate([_lane_pad(wi[:, o_f:o_z]), w_dt], axis=1).astype(BF16)
    col_k, col_v, col_z, col_x = aw, 2 * aw, 3 * aw, 3 * aw + sw

    meta_rows = jnp.concatenate([jnp.zeros((CHUNK - N_META, d), F32), meta_tokens.astype(F32)], axis=0)
    g_mix = norm_mix_g[lyr]
    n_meta = _rmsnorm_rows(meta_rows, g_mix, CHUNK)
    n_real = _rmsnorm_rows(x.reshape(tokens, d), g_mix, 512)
    big_meta = _matmul(n_meta, w_big, BF16, CHUNK, 1024, "inproj_meta")
    small_meta = _matmul(n_meta, w_small, F32, CHUNK, 640, "inproj_small_meta")
    big = _matmul(n_real, w_big, BF16, 1024, 1024, "inproj")
    small = _matmul(n_real, w_small, F32, 1024, 640, "inproj_small")

    fb = _lane_pad(fgate_b[lyr].reshape(1, n_ah).astype(F32))
    ct_meta = _fgate_cumsum(small_meta[:, :LANES], fb, jnp.zeros((LANES, 1), F32), 1, n_ah,
                            n_lead_pad=CHUNK - N_META)
    init_col = jnp.pad(ct_meta[0, :, CHUNK - 1], (0, LANES - n_ah)).reshape(LANES, 1)
    ct = _fgate_cumsum(small[:, :LANES], fb, init_col, n_batch, n_ah)
    n_pairs = n_ah // 2
    attn = _attention(big, big_meta, ct.reshape(n_batch, n_pairs, 2, seq), ct_meta.reshape(1, n_pairs, 2, CHUNK),
                      n_batch, seq, n_pairs, 0, col_k // LANES, col_v // LANES)

    head_params = jnp.zeros((SSM_GROUPS, 8, LANES), F32)
    head_params = head_params.at[:, 0, :hpg].set(dt_bias[lyr].reshape(SSM_GROUPS, hpg).astype(F32))
    head_params = head_params.at[:, 1, :hpg].set(a_log[lyr].reshape(SSM_GROUPS, hpg).astype(F32))
    dskip_x = jnp.repeat(d_skip[lyr].astype(F32), HEAD_DIM).reshape(1, sw)
    cw, cb = conv_w[lyr].astype(F32), conv_b[lyr].astype(F32).reshape(1, -1)
    dtg_meta, dtg = small_meta[:, LANES:], small[:, LANES:]
    pg = sw // SSM_GROUPS
    zeros_h = lambda w: jnp.zeros((1, HALO, w), F32)
    state0 = jnp.zeros((1, SSM_GROUPS, SSM_STATE, pg), F32)
    _, s_meta, hx, hb, hc = _ssd(big_meta, dtg_meta, cw, cb, head_params, dskip_x, zeros_h(sw), zeros_h(gn),
                                 zeros_h(gn), state0, 1, CHUNK, sw, col_z, col_x, n_lead_pad=CHUNK - N_META)
    yssm = _ssd(big, dtg, cw, cb, head_params, dskip_x, hx, hb, hc, s_meta, n_batch, seq, sw, col_z, col_x)[0]

    wr = _lane_pad(w_router[lyr]).astype(BF16)
    br = jnp.concatenate([b_router[lyr].astype(F32), jnp.full((LANES - N_EXPERTS,), NEG, F32)]).reshape(1, LANES)
    h1, n2, logits = _outproj(attn, yssm, x.reshape(tokens, d), attn_norm_g[lyr], ssm_norm_g[lyr],
                              w_out[lyr].astype(BF16), norm_ffn_g[lyr], wr, br)

    idx, gates, counts = _route(logits)
    tm = min(512, tokens)
    eid, rank = idx[:, :TOP_K], idx[:, TOP_K:2 * TOP_K]
    cnt = counts[0, :N_EXPERTS].astype(I32)
    padded = ((cnt + tm - 1) // tm) * tm
    ends = jnp.cumsum(padded)
    starts = ends - padded
    dest = (starts[eid] + rank).reshape(-1)
    n_tiles = (tokens * TOP_K + N_EXPERTS * (tm - 1)) // tm
    tile_expert = jnp.minimum(jnp.searchsorted(ends, jnp.arange(n_tiles, dtype=I32) * tm, side="right"),
                              N_EXPERTS - 1).astype(I32)
    n_used = (ends[-1] // tm).astype(I32).reshape(1)

    xs_sorted = _dispatch(dest, n2, n_tiles * tm)
    hdn = _grouped_gate_up(tile_expert, n_used, xs_sorted, w_gate_up[lyr].astype(BF16),
                           b_gate_up[lyr].astype(F32).reshape(N_EXPERTS, 1, -1), tm)
    ys = _grouped_down(tile_expert, n_used, hdn, w_down[lyr].astype(BF16),
                       b_down[lyr].astype(F32).reshape(N_EXPERTS, 1, -1), tm)
    out = _combine(dest, ys, gates, h1, final_norm_g)
    return out.reshape(n_batch, seq, d)
```

```python
import functools

import numpy as np
import jax
import jax.numpy as jnp
from jax import lax
from jax.experimental import pallas as pl
from jax.experimental.pallas import tpu as pltpu

F32 = jnp.float32
BF16 = jnp.bfloat16
I32 = jnp.int32

N_META = 16
CHUNK = 128
HEAD_DIM = 64
SSM_GROUPS = 4
SSM_STATE = 128
CONV_WIDTH = 4
N_EXPERTS = 32
TOP_K = 4
SWIGLU_LIMIT = 7.0
SWIGLU_ALPHA = 1.702
EPS = 1e-5
LANES = 128
HALO = 8
NEG = -1e30
N_SPLIT = 3
VMEM_LIMIT_BYTES = 56 * 1024 * 1024


def _cparams(*sem):
    return pltpu.CompilerParams(dimension_semantics=sem, vmem_limit_bytes=VMEM_LIMIT_BYTES)


def _sigmoid(x):
    return 1.0 / (1.0 + jnp.exp(-x))


def _softplus(x):
    return jnp.maximum(x, 0.0) + jnp.log(1.0 + jnp.exp(-jnp.abs(x)))


def _split_bf16(x, n=2):
    parts = []
    for _ in range(n):
        p = x.astype(BF16)
        parts.append(p)
        x = x - p.astype(F32)
    return parts


def _rmsnorm_kernel(x_ref, g_ref, o_ref):
    x = x_ref[...]
    ms = jnp.mean(x * x, axis=-1, keepdims=True)
    o_ref[...] = (x * lax.rsqrt(ms + EPS) * g_ref[...]).astype(o_ref.dtype)


def _rmsnorm_rows(x, g, tm):
    m, d = x.shape
    tm = min(tm, m)
    return pl.pallas_call(
        _rmsnorm_kernel,
        grid=(m // tm,),
        in_specs=[pl.BlockSpec((tm, d), lambda i: (i, 0)), pl.BlockSpec((1, d), lambda i: (0, 0))],
        out_specs=pl.BlockSpec((tm, d), lambda i: (i, 0)),
        out_shape=jax.ShapeDtypeStruct((m, d), BF16),
        compiler_params=_cparams("parallel"),
        name="rmsnorm_rows",
    )(x, g.reshape(1, d))


def _matmul_kernel(a_ref, w_ref, o_ref):
    o_ref[...] = jnp.dot(a_ref[...], w_ref[...], preferred_element_type=F32).astype(o_ref.dtype)


def _pick_tile(n, pref, unit=LANES):
    best = unit
    for t in range(unit, min(pref, n) + 1, unit):
        if n % t == 0:
            best = t
    return best


def _matmul(a, w, out_dtype, tm, tn, name):
    m, k = a.shape
    n = w.shape[1]
    tm, tn = min(tm, m), _pick_tile(n, tn)
    return pl.pallas_call(
        _matmul_kernel,
        grid=(n // tn, m // tm),
        in_specs=[pl.BlockSpec((tm, k), lambda j, i: (i, 0)), pl.BlockSpec((k, tn), lambda j, i: (0, j))],
        out_specs=pl.BlockSpec((tm, tn), lambda j, i: (i, j)),
        out_shape=jax.ShapeDtypeStruct((m, n), out_dtype),
        compiler_params=_cparams("parallel", "parallel"),
        name=name,
    )(a, w)


def _gate_lane_maps(n_heads):
    n_pairs = n_heads // 2
    pk = np.zeros((N_SPLIT * LANES, n_pairs * LANES), np.float32)
    pq = np.zeros_like(pk)
    ones_k = np.zeros((1, n_pairs * LANES), np.float32)
    ones_q = np.zeros_like(ones_k)
    for h in range(n_heads):
        base = (h // 2) * LANES + (HEAD_DIM if h % 2 == 0 else 0)
        for i in range(N_SPLIT):
            pk[i * LANES + h, base + i] = -1.0
            pq[i * LANES + h, base + N_SPLIT + i] = 1.0
            ones_q[0, base + i] = 1.0
            ones_k[0, base + N_SPLIT + i] = 1.0
    return (jnp.asarray(pk, BF16), jnp.asarray(pq, BF16), jnp.asarray(ones_k), jnp.asarray(ones_q))


def _fgate_cumsum_kernel(f_ref, b_ref, init_ref, pk_ref, pq_ref, ok_ref, oq_ref, xk_ref, yq_ref, last_ref,
                         carry_ref, *, blk, n_lead_pad):
    j = pl.program_id(1)

    @pl.when(j == 0)
    def _():
        carry_ref[...] = init_ref[...]

    x = f_ref[...] + b_ref[...]
    lf = jnp.minimum(x, 0.0) - jnp.log(1.0 + jnp.exp(-jnp.abs(x)))
    if n_lead_pad:
        row = lax.broadcasted_iota(I32, lf.shape, 0) + j * blk
        lf = jnp.where(row < n_lead_pad, 0.0, lf)
    hi, lo = _split_bf16(lf)
    lower = (lax.broadcasted_iota(I32, (blk, blk), 0) >= lax.broadcasted_iota(I32, (blk, blk), 1)).astype(BF16)
    c = (jnp.dot(lower, hi, preferred_element_type=F32) + jnp.dot(lower, lo, preferred_element_type=F32)
         + carry_ref[...])
    cc = jnp.concatenate(_split_bf16(c, N_SPLIT), axis=1)
    xk_ref[...] = (jnp.dot(cc, pk_ref[...], preferred_element_type=F32) + ok_ref[...]).astype(BF16)
    yq_ref[...] = (jnp.dot(cc, pq_ref[...], preferred_element_type=F32) + oq_ref[...]).astype(BF16)
    carry_ref[...] = c[blk - 1:blk, :]
    last_ref[0] = c[blk - 1:blk, :]


def _fgate_cumsum(f_small, bias_row, init_row, lane_maps, n_batch, n_lead_pad=0, blk=512):
    rows = f_small.shape[0]
    s = rows // n_batch
    blk = min(blk, s)
    nb = s // blk
    pk, pq, ones_k, ones_q = lane_maps
    width = pk.shape[1]
    const = lambda a: pl.BlockSpec(a.shape, lambda b, j: (0, 0))
    return pl.pallas_call(
        functools.partial(_fgate_cumsum_kernel, blk=blk, n_lead_pad=n_lead_pad),
        grid=(n_batch, nb),
        in_specs=[pl.BlockSpec((blk, LANES), lambda b, j: (b * nb + j, 0)),
                  pl.BlockSpec((1, LANES), lambda b, j: (0, 0)),
                  pl.BlockSpec((1, LANES), lambda b, j: (0, 0)),
                  const(pk), const(pq), const(ones_k), const(ones_q)],
        out_specs=[pl.BlockSpec((blk, width), lambda b, j: (b * nb + j, 0)),
                   pl.BlockSpec((blk, width), lambda b, j: (b * nb + j, 0)),
                   pl.BlockSpec((1, 1, LANES), lambda b, j: (b, 0, 0))],
        out_shape=[jax.ShapeDtypeStruct((rows, width), BF16), jax.ShapeDtypeStruct((rows, width), BF16),
                   jax.ShapeDtypeStruct((n_batch, 1, LANES), F32)],
        scratch_shapes=[pltpu.VMEM((1, LANES), F32)],
        compiler_params=_cparams("parallel", "arbitrary"),
        name="fgate_cumsum",
    )(f_small, bias_row, init_row, pk, pq, ones_k, ones_q)


def _attn_kernel(q_ref, k_ref, v_ref, xk_ref, yq_ref, km_ref, vm_ref, xkm_ref, o_ref,
                 kx_ref, vx_ref, m_ref, acc_ref, *, tq, n_q, n_meta_pad):
    lane = lax.broadcasted_iota(I32, (tq, LANES), 1)
    lo_half = lane < HEAD_DIM
    causal = lax.broadcasted_iota(I32, (tq, tq), 1) <= lax.broadcasted_iota(I32, (tq, tq), 0)
    meta_valid = lane >= n_meta_pad
    scale = jnp.asarray(HEAD_DIM ** -0.5, BF16)
    nt = (((1,), (1,)), ((), ()))

    def lo_mask(rows):
        return lax.broadcasted_iota(I32, (rows, LANES), 1) < HEAD_DIM

    k, v, xk = k_ref[...], v_ref[...], xk_ref[...]
    lo_seq = lo_mask(k.shape[0])
    ones_seq = jnp.full(v.shape, 1.0, BF16)
    kx_ref[0] = jnp.where(lo_seq, k, xk)
    kx_ref[1] = jnp.where(lo_seq, xk, k)
    vx_ref[0] = jnp.where(lo_seq, v, ones_seq)
    vx_ref[1] = jnp.where(lo_seq, ones_seq, v)
    km, vm, xkm = km_ref[...], vm_ref[...], xkm_ref[...]
    lo_meta = lo_mask(km.shape[0])
    ones_meta = jnp.full(vm.shape, 1.0, BF16)
    kmx = (jnp.where(lo_meta, km, xkm), jnp.where(lo_meta, xkm, km))
    vmx = (jnp.where(lo_meta, vm, ones_meta), jnp.where(lo_meta, ones_meta, vm))

    def q_block(qi, carry):
        q0 = pl.multiple_of(qi * tq, tq)
        q = q_ref[pl.ds(q0, tq), :] * scale
        yq = yq_ref[pl.ds(q0, tq), :]
        qx = (jnp.where(lo_half, q, yq), jnp.where(lo_half, yq, q))

        for h in range(2):
            t = jnp.where(meta_valid, lax.dot_general(qx[h], kmx[h], nt, preferred_element_type=F32), NEG)
            m0 = jnp.max(t, axis=-1, keepdims=True)
            m_ref[h] = m0
            acc_ref[h] = jnp.dot(jnp.exp(t - m0).astype(BF16), vmx[h], preferred_element_type=F32)

        def update(h, k0, mask):
            t = lax.dot_general(qx[h], kx_ref[h, pl.ds(k0, tq), :], nt, preferred_element_type=F32)
            if mask is not None:
                t = jnp.where(mask, t, NEG)
            m_prev = m_ref[h]
            m_new = jnp.maximum(m_prev, jnp.max(t, axis=-1, keepdims=True))
            p = jnp.exp(t - m_new).astype(BF16)
            acc_ref[h] = (jnp.exp(m_prev - m_new) * acc_ref[h]
                          + jnp.dot(p, vx_ref[h, pl.ds(k0, tq), :], preferred_element_type=F32))
            m_ref[h] = m_new

        def k_block(j, c):
            k0 = pl.multiple_of(j * tq, tq)
            update(0, k0, None)
            update(1, k0, None)
            return c

        lax.fori_loop(0, qi, k_block, 0)
        update(0, q0, causal)
        update(1, q0, causal)

        a0, a1 = acc_ref[0], acc_ref[1]
        out = jnp.where(lo_half, a0 / pltpu.roll(a0, HEAD_DIM, 1), a1 / pltpu.roll(a1, HEAD_DIM, 1))
        o_ref[pl.ds(q0, tq), :] = out.astype(o_ref.dtype)
        return carry

    lax.fori_loop(0, n_q, q_block, 0)


def _attention(big, big_meta, xk, yq, xk_meta, n_batch, seq, n_pairs, col_q, col_k, col_v, tq=512):
    tq = min(tq, seq)
    n_q = seq // tq
    mrows = big_meta.shape[0]
    kern = functools.partial(_attn_kernel, tq=tq, n_q=n_q, n_meta_pad=mrows - N_META)
    blk = lambda c0: pl.BlockSpec((seq, LANES), lambda b, p, c0=c0: (b, c0 + p))
    mblk = lambda c0: pl.BlockSpec((mrows, LANES), lambda b, p, c0=c0: (0, c0 + p))
    return pl.pallas_call(
        kern,
        grid=(n_batch, n_pairs),
        in_specs=[blk(col_q), blk(col_k), blk(col_v), blk(0), blk(0), mblk(col_k), mblk(col_v), mblk(0)],
        out_specs=pl.BlockSpec((seq, LANES), lambda b, p: (b, p)),
        out_shape=jax.ShapeDtypeStruct((n_batch * seq, n_pairs * LANES), BF16),
        scratch_shapes=[pltpu.VMEM((2, seq, LANES), BF16), pltpu.VMEM((2, seq, LANES), BF16),
                        pltpu.VMEM((2, tq, 1), F32), pltpu.VMEM((2, tq, LANES), F32)],
        compiler_params=_cparams("parallel", "parallel"),
        name="fox_attention",
    )(big, big, big, xk, yq, big_meta, big_meta, xk_meta)


def _ssd_kernel(x_ref, b_ref, c_ref, z_ref, dt_ref, wx_ref, wb_ref, wc_ref, bx_ref, bb_ref, bc_ref,
                hp_ref, dsk_ref, hx0_ref, hb0_ref, hc0_ref, s0_ref,
                y_ref, sout_ref, hxo_ref, hbo_ref, hco_ref,
                xpad_ref, bpad_ref, cpad_ref, state_ref, *, n_pairs, n_lead_pad):
    c = pl.program_id(2)
    n_c = pl.num_programs(2)
    L = CHUNK

    @pl.when(c == 0)
    def _():
        xpad_ref[0:HALO, :] = hx0_ref[0].astype(F32)
        bpad_ref[0:HALO, :] = hb0_ref[0].astype(F32)
        cpad_ref[0:HALO, :] = hc0_ref[0].astype(F32)
        state_ref[...] = s0_ref[0, 0]

    def conv_silu(pad_ref, raw_ref, w_ref, bias_ref):
        pad_ref[HALO:HALO + L, :] = raw_ref[...].astype(F32)
        acc = bias_ref[...]
        for kk in range(CONV_WIDTH):
            off = HALO - (CONV_WIDTH - 1) + kk
            acc = acc + w_ref[kk:kk + 1, :] * pad_ref[off:off + L, :]
        return acc * _sigmoid(acc)

    xs = conv_silu(xpad_ref, x_ref, wx_ref, bx_ref)
    bm = conv_silu(bpad_ref, b_ref, wb_ref, bb_ref)
    cm = conv_silu(cpad_ref, c_ref, wc_ref, bc_ref)

    hp = hp_ref[0]
    dt = _softplus(dt_ref[...] + hp[0:1, :])
    if n_lead_pad:
        row = lax.broadcasted_iota(I32, dt.shape, 0) + c * L
        dt = jnp.where(row < n_lead_pad, 0.0, dt)
    a = dt * (-jnp.exp(hp[1:2, :]))

    ri = lax.broadcasted_iota(I32, (L, L), 0)
    ci = lax.broadcasted_iota(I32, (L, L), 1)
    tri = ri >= ci
    tri_bf = tri.astype(BF16)
    a_hi, a_lo = _split_bf16(a)
    a_cs = jnp.dot(tri_bf, a_hi, preferred_element_type=F32) + jnp.dot(tri_bf, a_lo, preferred_element_type=F32)
    a_cst = a_cs.T
    a_last = a_cs[L - 1:L, :]
    ea = jnp.exp(a_cs)
    wdec = jnp.exp(a_last - a_cs)
    etot = jnp.exp(a_last)

    cm_bf = cm.astype(BF16)
    bm_bf = bm.astype(BF16)
    cb = lax.dot_general(cm_bf, bm_bf, (((1,), (1,)), ((), ())), preferred_element_type=F32)
    y_off = jnp.dot(cm_bf, state_ref[...].astype(BF16), preferred_element_type=F32)

    lane = lax.broadcasted_iota(I32, (L, LANES), 1)
    lo_half = lane < HEAD_DIM
    lo_row = lo_half[0:1, :]
    xw_parts = []
    etot_parts = []
    for p in range(n_pairs):
        ha, hb = 2 * p, 2 * p + 1
        sl = slice(p * LANES, (p + 1) * LANES)
        pick = lambda arr: jnp.where(lo_half, arr[:, ha:ha + 1], arr[:, hb:hb + 1])
        xs_p = xs[:, sl]
        xdt = xs_p * pick(dt)
        xdt_bf = xdt.astype(BF16)
        y_heads = []
        for hh in (ha, hb):
            seg = a_cs[:, hh:hh + 1] - a_cst[hh:hh + 1, :]
            dec = jnp.where(tri, jnp.exp(seg), 0.0)
            y_heads.append(jnp.dot((cb * dec).astype(BF16), xdt_bf, preferred_element_type=F32))
        y = jnp.where(lo_half, y_heads[0], y_heads[1]) + y_off[:, sl] * pick(ea) + dsk_ref[:, sl] * xs_p
        zf = z_ref[:, sl].astype(F32)
        y_ref[:, sl] = (y * (zf * _sigmoid(zf))).astype(y_ref.dtype)
        xw_parts.append((xdt * pick(wdec)).astype(BF16))
        etot_parts.append(jnp.where(lo_row, etot[:, ha:ha + 1], etot[:, hb:hb + 1]))

    xw = xw_parts[0] if n_pairs == 1 else jnp.concatenate(xw_parts, axis=1)
    etot_x = etot_parts[0] if n_pairs == 1 else jnp.concatenate(etot_parts, axis=1)
    state_ref[...] = state_ref[...] * etot_x + jnp.dot(bm.T.astype(BF16), xw, preferred_element_type=F32)

    xpad_ref[0:HALO, :] = xpad_ref[L:L + HALO, :]
    bpad_ref[0:HALO, :] = bpad_ref[L:L + HALO, :]
    cpad_ref[0:HALO, :] = cpad_ref[L:L + HALO, :]

    @pl.when(c == n_c - 1)
    def _():
        sout_ref[0, 0] = state_ref[...]
        hxo_ref[0] = xpad_ref[0:HALO, :]
        hbo_ref[0] = bpad_ref[0:HALO, :]
        hco_ref[0] = cpad_ref[0:HALO, :]


def _ssd(big, dtg, conv_w, conv_b, head_params, dskip_x, halo_x, halo_b, halo_c, state0,
         n_batch, seq, d_inner, col_z, col_x, n_lead_pad=0):
    G, N, L = SSM_GROUPS, SSM_STATE, CHUNK
    pg = d_inner // G
    n_pairs = pg // LANES
    n_c = seq // L
    cx, cz = col_x // pg, col_z // pg
    cb0 = (col_x + d_inner) // N
    cc0 = cb0 + G
    row = lambda b, g, c: b * n_c + c
    kern = functools.partial(_ssd_kernel, n_pairs=n_pairs, n_lead_pad=n_lead_pad)
    in_specs = [
        pl.BlockSpec((L, pg), lambda b, g, c: (row(b, g, c), cx + g)),
        pl.BlockSpec((L, N), lambda b, g, c: (row(b, g, c), cb0 + g)),
        pl.BlockSpec((L, N), lambda b, g, c: (row(b, g, c), cc0 + g)),
        pl.BlockSpec((L, pg), lambda b, g, c: (row(b, g, c), cz + g)),
        pl.BlockSpec((L, LANES), lambda b, g, c: (row(b, g, c), g)),
        pl.BlockSpec((CONV_WIDTH, pg), lambda b, g, c: (0, g)),
        pl.BlockSpec((CONV_WIDTH, N), lambda b, g, c: (0, d_inner // N + g)),
        pl.BlockSpec((CONV_WIDTH, N), lambda b, g, c: (0, d_inner // N + G + g)),
        pl.BlockSpec((1, pg), lambda b, g, c: (0, g)),
        pl.BlockSpec((1, N), lambda b, g, c: (0, d_inner // N + g)),
        pl.BlockSpec((1, N), lambda b, g, c: (0, d_inner // N + G + g)),
        pl.BlockSpec((1, 8, LANES), lambda b, g, c: (g, 0, 0)),
        pl.BlockSpec((1, pg), lambda b, g, c: (0, g)),
        pl.BlockSpec((1, HALO, pg), lambda b, g, c: (0, 0, g)),
        pl.BlockSpec((1, HALO, N), lambda b, g, c: (0, 0, g)),
        pl.BlockSpec((1, HALO, N), lambda b, g, c: (0, 0, g)),
        pl.BlockSpec((1, 1, N, pg), lambda b, g, c: (0, g, 0, 0)),
    ]
    out_specs = [
        pl.BlockSpec((L, pg), lambda b, g, c: (row(b, g, c), g)),
        pl.BlockSpec((1, 1, N, pg), lambda b, g, c: (b, g, 0, 0)),
        pl.BlockSpec((1, HALO, pg), lambda b, g, c: (b, 0, g)),
        pl.BlockSpec((1, HALO, N), lambda b, g, c: (b, 0, g)),
        pl.BlockSpec((1, HALO, N), lambda b, g, c: (b, 0, g)),
    ]
    out_shape = [
        jax.ShapeDtypeStruct((n_batch * seq, d_inner), F32),
        jax.ShapeDtypeStruct((n_batch, G, N, pg), F32),
        jax.ShapeDtypeStruct((n_batch, HALO, d_inner), F32),
        jax.ShapeDtypeStruct((n_batch, HALO, G * N), F32),
        jax.ShapeDtypeStruct((n_batch, HALO, G * N), F32),
    ]
    return pl.pallas_call(
        kern,
        grid=(n_batch, G, n_c),
        in_specs=in_specs,
        out_specs=out_specs,
        out_shape=out_shape,
        scratch_shapes=[pltpu.VMEM((L + HALO, pg), F32), pltpu.VMEM((L + HALO, N), F32),
                        pltpu.VMEM((L + HALO, N), F32), pltpu.VMEM((N, pg), F32)],
        compiler_params=_cparams("parallel", "parallel", "arbitrary"),
        name="ssd_scan",
    )(big, big, big, big, dtg, conv_w, conv_w, conv_w, conv_b, conv_b, conv_b,
      head_params, dskip_x, halo_x, halo_b, halo_c, state0)


def _outproj_kernel(attn_ref, y_ref, x_ref, ga_ref, gs_ref, w_ref, gf_ref, wr_ref, br_ref,
                    h1_ref, n2_ref, logit_ref, *, attn_width):
    def norm(v, g):
        return v * lax.rsqrt(jnp.mean(v * v, axis=-1, keepdims=True) + EPS) * g

    an = norm(attn_ref[...].astype(F32), ga_ref[...]).astype(BF16)
    yn = norm(y_ref[...], gs_ref[...]).astype(BF16)
    mixed = (jnp.dot(an, w_ref[0:attn_width, :], preferred_element_type=F32)
             + jnp.dot(yn, w_ref[attn_width:, :], preferred_element_type=F32))
    h1 = x_ref[...] + mixed
    n2 = norm(h1, gf_ref[...])
    h1_ref[...] = h1
    n2_ref[...] = n2
    logit_ref[...] = jnp.dot(n2.astype(BF16), wr_ref[...], preferred_element_type=F32) + br_ref[...]


def _outproj(attn, yssm, x, g_attn, g_ssm, w_out, g_ffn, w_router, b_router, tm=512):
    m, d = x.shape
    aw, sw = attn.shape[1], yssm.shape[1]
    tm = min(tm, m)
    row = lambda w: pl.BlockSpec((tm, w), lambda i: (i, 0))
    full = lambda r, c: pl.BlockSpec((r, c), lambda i: (0, 0))
    return pl.pallas_call(
        functools.partial(_outproj_kernel, attn_width=aw),
        grid=(m // tm,),
        in_specs=[row(aw), row(sw), row(d), full(1, aw), full(1, sw), full(aw + sw, d), full(1, d),
                  full(d, LANES), full(1, LANES)],
        out_specs=[row(d), row(d), row(LANES)],
        out_shape=[jax.ShapeDtypeStruct((m, d), F32), jax.ShapeDtypeStruct((m, d), F32),
                   jax.ShapeDtypeStruct((m, LANES), F32)],
        compiler_params=_cparams("parallel"),
        name="outproj_router",
    )(attn, yssm, x, g_attn.reshape(1, aw), g_ssm.reshape(1, sw), w_out, g_ffn.reshape(1, d),
      w_router, b_router)


def _route_kernel(logit_ref, idx_ref, gate_ref, cnt_ref, carry_ref, *, tm):
    i = pl.program_id(0)

    @pl.when(i == 0)
    def _():
        carry_ref[...] = jnp.zeros_like(carry_ref)

    work = logit_ref[...]
    lane = lax.broadcasted_iota(I32, work.shape, 1)
    lane_f = lane.astype(F32)
    vals, ids, hots = [], [], []
    for _ in range(TOP_K):
        mk = jnp.max(work, axis=-1, keepdims=True)
        ik = jnp.min(jnp.where(work == mk, lane_f, float(LANES)), axis=-1, keepdims=True)
        hot = lane_f == ik
        work = jnp.where(hot, -jnp.inf, work)
        vals.append(mk)
        ids.append(ik)
        hots.append(hot)
    exps = [jnp.exp(v - vals[0]) for v in vals]
    denom = exps[0] + exps[1] + exps[2] + exps[3]
    sel = jnp.zeros(work.shape, F32)
    for hot in hots:
        sel = sel + hot.astype(F32)
    strict = (lax.broadcasted_iota(I32, (tm, tm), 1) < lax.broadcasted_iota(I32, (tm, tm), 0)).astype(BF16)
    before = jnp.dot(strict, sel.astype(BF16), preferred_element_type=F32) + carry_ref[...]
    idx_out = jnp.zeros(work.shape, I32)
    gate_out = jnp.zeros(work.shape, F32)
    for k in range(TOP_K):
        rank = jnp.sum(jnp.where(hots[k], before, 0.0), axis=-1, keepdims=True).astype(I32)
        idx_out = jnp.where(lane == k, ids[k].astype(I32), idx_out)
        idx_out = jnp.where(lane == TOP_K + k, rank, idx_out)
        gate_out = jnp.where(lane == k, exps[k] / denom, gate_out)
    idx_ref[...] = idx_out
    gate_ref[...] = gate_out
    carry_ref[...] = carry_ref[...] + jnp.sum(sel, axis=0, keepdims=True)
    cnt_ref[...] = carry_ref[...]


def _route(logits, tm=512):
    m = logits.shape[0]
    tm = min(tm, m)
    return pl.pallas_call(
        functools.partial(_route_kernel, tm=tm),
        grid=(m // tm,),
        in_specs=[pl.BlockSpec((tm, LANES), lambda i: (i, 0))],
        out_specs=[pl.BlockSpec((tm, LANES), lambda i: (i, 0)), pl.BlockSpec((tm, LANES), lambda i: (i, 0)),
                   pl.BlockSpec((1, LANES), lambda i: (0, 0))],
        out_shape=[jax.ShapeDtypeStruct((m, LANES), I32), jax.ShapeDtypeStruct((m, LANES), F32),
                   jax.ShapeDtypeStruct((1, LANES), F32)],
        scratch_shapes=[pltpu.VMEM((1, LANES), F32)],
        compiler_params=_cparams("arbitrary"),
        name="route_top4",
    )(logits)


def _dispatch_kernel(dest_ref, src_ref, init_ref, out_ref, sem, *, td):
    del init_ref

    def issue(j, c):
        for k in range(TOP_K):
            pltpu.make_async_copy(src_ref.at[pl.ds(j, 1)], out_ref.at[pl.ds(dest_ref[j * TOP_K + k], 1)],
                                  sem).start()
        return c

    lax.fori_loop(0, td, issue, 0, unroll=8)
    for _ in range(TOP_K):
        pltpu.make_async_copy(src_ref, out_ref.at[pl.ds(0, td)], sem).wait()


def _dispatch(dest_flat, src, n_rows_out, td=256):
    m, d = src.shape
    td = min(td, m)
    init = jnp.zeros((n_rows_out, d), src.dtype)
    return pl.pallas_call(
        functools.partial(_dispatch_kernel, td=td),
        grid=(m // td,),
        in_specs=[pl.BlockSpec((td * TOP_K,), lambda i: (i,), memory_space=pltpu.SMEM),
                  pl.BlockSpec((td, d), lambda i: (i, 0)), pl.BlockSpec(memory_space=pl.ANY)],
        out_specs=pl.BlockSpec(memory_space=pl.ANY),
        out_shape=jax.ShapeDtypeStruct((n_rows_out, d), src.dtype),
        scratch_shapes=[pltpu.SemaphoreType.DMA],
        input_output_aliases={2: 0},
        compiler_params=_cparams("arbitrary"),
        name="moe_dispatch",
    )(dest_flat, src, init)


def _expert_changed(te_ref):
    m = pl.program_id(1)
    return jnp.logical_or(m == 0, te_ref[m] != te_ref[jnp.maximum(m - 1, 0)])


def _gate_up_kernel(te_ref, nu_ref, x_ref, wg_ref, wu_ref, bg_ref, bu_ref, o_ref, wgb_ref, wub_ref):
    @pl.when(_expert_changed(te_ref))
    def _():
        wgb_ref[...] = wg_ref[0].astype(BF16)
        wub_ref[...] = wu_ref[0].astype(BF16)

    @pl.when(pl.program_id(1) < nu_ref[0])
    def _():
        x = x_ref[...].astype(BF16)
        g = jnp.dot(x, wgb_ref[...], preferred_element_type=F32) + bg_ref[0]
        u = jnp.dot(x, wub_ref[...], preferred_element_type=F32) + bu_ref[0]
        g = jnp.minimum(g, SWIGLU_LIMIT)
        u = jnp.clip(u, -SWIGLU_LIMIT, SWIGLU_LIMIT)
        o_ref[...] = (g * _sigmoid(SWIGLU_ALPHA * g) * (u + 1.0)).astype(o_ref.dtype)

    @pl.when(pl.program_id(1) >= nu_ref[0])
    def _():
        o_ref[...] = jnp.zeros_like(o_ref)


def _grouped_gate_up(tile_expert, n_used, xs, w_gu, b_gu, tm, tn=512):
    mp, d = xs.shape
    ff = w_gu.shape[2] // 2
    tn = min(tn, ff)
    nt = ff // tn
    grid_spec = pltpu.PrefetchScalarGridSpec(
        num_scalar_prefetch=2,
        grid=(nt, mp // tm),
        in_specs=[pl.BlockSpec((tm, d), lambda n, m, te, nu: (m, 0)),
                  pl.BlockSpec((1, d, tn), lambda n, m, te, nu: (te[m], 0, n)),
                  pl.BlockSpec((1, d, tn), lambda n, m, te, nu: (te[m], 0, nt + n)),
                  pl.BlockSpec((1, 1, tn), lambda n, m, te, nu: (te[m], 0, n)),
                  pl.BlockSpec((1, 1, tn), lambda n, m, te, nu: (te[m], 0, nt + n))],
        out_specs=pl.BlockSpec((tm, tn), lambda n, m, te, nu: (m, n)),
        scratch_shapes=[pltpu.VMEM((d, tn), BF16), pltpu.VMEM((d, tn), BF16)],
    )
    return pl.pallas_call(
        _gate_up_kernel,
        grid_spec=grid_spec,
        out_shape=jax.ShapeDtypeStruct((mp, ff), BF16),
        compiler_params=_cparams("arbitrary", "arbitrary"),
        name="moe_gate_up",
    )(tile_expert, n_used, xs, w_gu, w_gu, b_gu, b_gu)


def _down_kernel(te_ref, nu_ref, h_ref, w_ref, b_ref, o_ref, wb_ref):
    @pl.when(_expert_changed(te_ref))
    def _():
        wb_ref[...] = w_ref[0].astype(BF16)

    @pl.when(pl.program_id(1) < nu_ref[0])
    def _():
        o_ref[...] = jnp.dot(h_ref[...], wb_ref[...], preferred_element_type=F32) + b_ref[0]

    @pl.when(pl.program_id(1) >= nu_ref[0])
    def _():
        o_ref[...] = jnp.zeros_like(o_ref)


def _grouped_down(tile_expert, n_used, hdn, w_d, b_d, tm, tn=1024):
    mp, ff = hdn.shape
    d = w_d.shape[2]
    tn = min(tn, d)
    grid_spec = pltpu.PrefetchScalarGridSpec(
        num_scalar_prefetch=2,
        grid=(d // tn, mp // tm),
        in_specs=[pl.BlockSpec((tm, ff), lambda n, m, te, nu: (m, 0)),
                  pl.BlockSpec((1, ff, tn), lambda n, m, te, nu: (te[m], 0, n)),
                  pl.BlockSpec((1, 1, tn), lambda n, m, te, nu: (te[m], 0, n))],
        out_specs=pl.BlockSpec((tm, tn), lambda n, m, te, nu: (m, n)),
        scratch_shapes=[pltpu.VMEM((ff, tn), BF16)],
    )
    return pl.pallas_call(
        _down_kernel,
        grid_spec=grid_spec,
        out_shape=jax.ShapeDtypeStruct((mp, d), F32),
        compiler_params=_cparams("arbitrary", "arbitrary"),
        name="moe_down",
    )(tile_expert, n_used, hdn, w_d, b_d)


def _combine_kernel(dest_ref, ys_ref, gate_ref, h1_ref, g_ref, o_ref, buf_ref, sem, *, tc):
    def issue(j, c):
        for k in range(TOP_K):
            pltpu.make_async_copy(ys_ref.at[pl.ds(dest_ref[j * TOP_K + k], 1)], buf_ref.at[k, pl.ds(j, 1)],
                                  sem).start()
        return c

    lax.fori_loop(0, tc, issue, 0, unroll=8)
    for k in range(TOP_K):
        pltpu.make_async_copy(ys_ref.at[pl.ds(0, tc)], buf_ref.at[k], sem).wait()

    gates = gate_ref[...]
    h2 = h1_ref[...]
    for k in range(TOP_K):
        h2 = h2 + gates[:, k:k + 1] * buf_ref[k]
    o_ref[...] = h2 * lax.rsqrt(jnp.mean(h2 * h2, axis=-1, keepdims=True) + EPS) * g_ref[...]


def _combine(dest_flat, ys, gates, h1, g_final, tc=256):
    m, d = h1.shape
    tc = min(tc, m)
    return pl.pallas_call(
        functools.partial(_combine_kernel, tc=tc),
        grid=(m // tc,),
        in_specs=[pl.BlockSpec((tc * TOP_K,), lambda i: (i,), memory_space=pltpu.SMEM),
                  pl.BlockSpec(memory_space=pl.ANY),
                  pl.BlockSpec((tc, LANES), lambda i: (i, 0)),
                  pl.BlockSpec((tc, d), lambda i: (i, 0)),
                  pl.BlockSpec((1, d), lambda i: (0, 0))],
        out_specs=pl.BlockSpec((tc, d), lambda i: (i, 0)),
        out_shape=jax.ShapeDtypeStruct((m, d), F32),
        scratch_shapes=[pltpu.VMEM((TOP_K, tc, d), F32), pltpu.SemaphoreType.DMA],
        compiler_params=_cparams("arbitrary"),
        name="moe_combine",
    )(dest_flat, ys, gates, h1, g_final.reshape(1, d))


def _lane_pad(v, width=LANES):
    return jnp.pad(v, ((0, 0), (0, width - v.shape[1])))


def kernel(x, meta_tokens, norm_mix_g, w_in, fgate_b, attn_norm_g, conv_w, conv_b, dt_bias, a_log, d_skip,
           ssm_norm_g, w_out, norm_ffn_g, w_router, b_router, w_gate_up, b_gate_up, w_down, b_down,
           final_norm_g):
    n_batch, seq, d = x.shape
    aw = (d // 128) * HEAD_DIM
    n_ah = aw // HEAD_DIM
    sw = d
    n_sh = sw // HEAD_DIM
    hpg = n_sh // SSM_GROUPS
    gn = SSM_GROUPS * SSM_STATE
    tokens = n_batch * seq
    lyr = 0

    o_f = 3 * aw
    o_z = o_f + n_ah
    o_xbc = o_z + sw
    o_dt = o_xbc + sw + 2 * gn
    wi = w_in[lyr]
    w_big = jnp.concatenate([wi[:, :o_f], wi[:, o_z:o_dt]], axis=1).astype(BF16)
    w_dt = wi[:, o_dt:o_dt + n_sh].reshape(d, SSM_GROUPS, hpg)
    w_dt = jnp.pad(w_dt, ((0, 0), (0, 0), (0, LANES - hpg))).reshape(d, SSM_GROUPS * LANES)
    w_small = jnp.concatenate([_lane_pad(wi[:, o_f:o_z]), w_dt], axis=1).astype(BF16)
    col_k, col_v, col_z, col_x = aw, 2 * aw, 3 * aw, 3 * aw + sw

    meta_rows = jnp.concatenate([jnp.zeros((CHUNK - N_META, d), F32), meta_tokens.astype(F32)], axis=0)
    g_mix = norm_mix_g[lyr]
    n_meta = _rmsnorm_rows(meta_rows, g_mix, CHUNK)
    n_real = _rmsnorm_rows(x.reshape(tokens, d), g_mix, 512)
    big_meta = _matmul(n_meta, w_big, BF16, CHUNK, 1024, "inproj_meta")
    small_meta = _matmul(n_meta, w_small, F32, CHUNK, 640, "inproj_small_meta")
    big = _matmul(n_real, w_big, BF16, 1024, 1024, "inproj")
    small = _matmul(n_real, w_small, F32, 1024, 640, "inproj_small")

    fb = _lane_pad(fgate_b[lyr].reshape(1, n_ah).astype(F32))
    lane_maps = _gate_lane_maps(n_ah)
    xk_meta, _, c_meta_last = _fgate_cumsum(small_meta[:, :LANES], fb, jnp.zeros((1, LANES), F32), lane_maps, 1,
                                            n_lead_pad=CHUNK - N_META)
    xk, yq, _ = _fgate_cumsum(small[:, :LANES], fb, c_meta_last[0], lane_maps, n_batch)
    attn = _attention(big, big_meta, xk, yq, xk_meta, n_batch, seq, n_ah // 2, 0, col_k // LANES,
                      col_v // LANES)

    head_params = jnp.zeros((SSM_GROUPS, 8, LANES), F32)
    head_params = head_params.at[:, 0, :hpg].set(dt_bias[lyr].reshape(SSM_GROUPS, hpg).astype(F32))
    head_params = head_params.at[:, 1, :hpg].set(a_log[lyr].reshape(SSM_GROUPS, hpg).astype(F32))
    dskip_x = jnp.repeat(d_skip[lyr].astype(F32), HEAD_DIM).reshape(1, sw)
    cw, cb = conv_w[lyr].astype(F32), conv_b[lyr].astype(F32).reshape(1, -1)
    dtg_meta, dtg = small_meta[:, LANES:], small[:, LANES:]
    pg = sw // SSM_GROUPS
    zeros_h = lambda w: jnp.zeros((1, HALO, w), F32)
    state0 = jnp.zeros((1, SSM_GROUPS, SSM_STATE, pg), F32)
    _, s_meta, hx, hb, hc = _ssd(big_meta, dtg_meta, cw, cb, head_params, dskip_x, zeros_h(sw), zeros_h(gn),
                                 zeros_h(gn), state0, 1, CHUNK, sw, col_z, col_x, n_lead_pad=CHUNK - N_META)
    yssm = _ssd(big, dtg, cw, cb, head_params, dskip_x, hx, hb, hc, s_meta, n_batch, seq, sw, col_z, col_x)[0]

    wr = _lane_pad(w_router[lyr]).astype(BF16)
    br = jnp.concatenate([b_router[lyr].astype(F32), jnp.full((LANES - N_EXPERTS,), NEG, F32)]).reshape(1, LANES)
    h1, n2, logits = _outproj(attn, yssm, x.reshape(tokens, d), attn_norm_g[lyr], ssm_norm_g[lyr],
                              w_out[lyr].astype(BF16), norm_ffn_g[lyr], wr, br)

    idx, gates, counts = _route(logits)
    tm = min(512, tokens)
    eid, rank = idx[:, :TOP_K], idx[:, TOP_K:2 * TOP_K]
    cnt = counts[0, :N_EXPERTS].astype(I32)
    padded = ((cnt + tm - 1) // tm) * tm
    ends = jnp.cumsum(padded)
    starts = ends - padded
    dest = (starts[eid] + rank).reshape(-1)
    n_tiles = (tokens * TOP_K + N_EXPERTS * (tm - 1)) // tm
    tile_row0 = jnp.arange(n_tiles, dtype=I32) * tm
    tile_expert = jnp.minimum(jnp.sum((ends[None, :] <= tile_row0[:, None]).astype(I32), axis=1), N_EXPERTS - 1)
    n_used = (ends[-1] // tm).astype(I32).reshape(1)

    xs_sorted = _dispatch(dest, n2, n_tiles * tm)
    hdn = _grouped_gate_up(tile_expert, n_used, xs_sorted, w_gate_up[lyr],
                           b_gate_up[lyr].astype(F32).reshape(N_EXPERTS, 1, -1), tm)
    ys = _grouped_down(tile_expert, n_used, hdn, w_down[lyr], b_down[lyr].astype(F32).reshape(N_EXPERTS, 1, -1), tm)
    out = _combine(dest, ys, gates, h1, final_norm_g)
    return out.reshape(n_batch, seq, d)
```

```python
import functools

import numpy as np
import jax
import jax.numpy as jnp
from jax import lax
from jax.experimental import pallas as pl
from jax.experimental.pallas import tpu as pltpu

F32 = jnp.float32
BF16 = jnp.bfloat16
I32 = jnp.int32
U32 = jnp.uint32

N_META = 16
CHUNK = 128
HEAD_DIM = 64
SSM_GROUPS = 4
SSM_STATE = 128
CONV_WIDTH = 4
N_EXPERTS = 32
TOP_K = 4
SWIGLU_LIMIT = 7.0
SWIGLU_ALPHA = 1.702
EPS = 1e-5
LANES = 128
HALO = 8
NEG = -1e30
LOG2E = 1.4426950408889634
N_SPLIT = 3
VMEM_LIMIT_BYTES = 56 * 1024 * 1024


def _cparams(*sem):
    return pltpu.CompilerParams(dimension_semantics=sem, vmem_limit_bytes=VMEM_LIMIT_BYTES)


def _sigmoid(x):
    return 1.0 / (1.0 + jnp.exp(-x))


def _softplus(x):
    return jnp.maximum(x, 0.0) + jnp.log(1.0 + jnp.exp(-jnp.abs(x)))


def _pack_bf16_pair(lo, hi):
    return pltpu.pack_elementwise([lo, hi], packed_dtype=BF16)


def _unpack_bf16_pair(w):
    lo = pltpu.unpack_elementwise(w, index=0, packed_dtype=BF16, unpacked_dtype=F32)
    hi = pltpu.unpack_elementwise(w, index=1, packed_dtype=BF16, unpacked_dtype=F32)
    return lo, hi


def _split_bf16(x, n=2):
    parts = []
    for _ in range(n):
        p = x.astype(BF16)
        parts.append(p)
        x = x - p.astype(F32)
    return parts


def _rmsnorm_kernel(x_ref, g_ref, o_ref):
    x = x_ref[...]
    ms = jnp.mean(x * x, axis=-1, keepdims=True)
    o_ref[...] = (x * lax.rsqrt(ms + EPS) * g_ref[...]).astype(o_ref.dtype)


def _rmsnorm_rows(x, g, tm):
    m, d = x.shape
    tm = min(tm, m)
    return pl.pallas_call(
        _rmsnorm_kernel,
        grid=(m // tm,),
        in_specs=[pl.BlockSpec((tm, d), lambda i: (i, 0)), pl.BlockSpec((1, d), lambda i: (0, 0))],
        out_specs=pl.BlockSpec((tm, d), lambda i: (i, 0)),
        out_shape=jax.ShapeDtypeStruct((m, d), BF16),
        compiler_params=_cparams("parallel"),
        name="rmsnorm_rows",
    )(x, g.reshape(1, d))


def _matmul_kernel(a_ref, w_ref, o_ref):
    o_ref[...] = jnp.dot(a_ref[...], w_ref[...], preferred_element_type=F32).astype(o_ref.dtype)


def _pick_tile(n, pref, unit=LANES):
    best = unit
    for t in range(unit, min(pref, n) + 1, unit):
        if n % t == 0:
            best = t
    return best


def _matmul(a, w, out_dtype, tm, tn, name):
    m, k = a.shape
    n = w.shape[1]
    tm, tn = min(tm, m), _pick_tile(n, tn)
    return pl.pallas_call(
        _matmul_kernel,
        grid=(n // tn, m // tm),
        in_specs=[pl.BlockSpec((tm, k), lambda j, i: (i, 0)), pl.BlockSpec((k, tn), lambda j, i: (0, j))],
        out_specs=pl.BlockSpec((tm, tn), lambda j, i: (i, j)),
        out_shape=jax.ShapeDtypeStruct((m, n), out_dtype),
        compiler_params=_cparams("parallel", "parallel"),
        name=name,
    )(a, w)


def _gate_lane_maps(n_heads):
    n_pairs = n_heads // 2
    pk = np.zeros((N_SPLIT * LANES, n_pairs * LANES), np.float32)
    pq = np.zeros_like(pk)
    ones_k = np.zeros((1, n_pairs * LANES), np.float32)
    ones_q = np.zeros_like(ones_k)
    for h in range(n_heads):
        base = (h // 2) * LANES + (HEAD_DIM if h % 2 == 0 else 0)
        for i in range(N_SPLIT):
            pk[i * LANES + h, base + i] = -1.0
            pq[i * LANES + h, base + N_SPLIT + i] = 1.0
            ones_q[0, base + i] = 1.0
            ones_k[0, base + N_SPLIT + i] = 1.0
    return (jnp.asarray(pk, BF16), jnp.asarray(pq, BF16), jnp.asarray(ones_k), jnp.asarray(ones_q))


def _fgate_cumsum_kernel(f_ref, b_ref, init_ref, pk_ref, pq_ref, ok_ref, oq_ref, xk_ref, yq_ref, last_ref,
                         carry_ref, *, blk, n_lead_pad):
    j = pl.program_id(1)

    @pl.when(j == 0)
    def _():
        carry_ref[...] = init_ref[...]

    x = f_ref[...] + b_ref[...]
    lf = jnp.minimum(x, 0.0) - jnp.log(1.0 + jnp.exp(-jnp.abs(x)))
    if n_lead_pad:
        row = lax.broadcasted_iota(I32, lf.shape, 0) + j * blk
        lf = jnp.where(row < n_lead_pad, 0.0, lf)
    hi, lo = _split_bf16(lf)
    lower = (lax.broadcasted_iota(I32, (blk, blk), 0) >= lax.broadcasted_iota(I32, (blk, blk), 1)).astype(BF16)
    c = (jnp.dot(lower, hi, preferred_element_type=F32) + jnp.dot(lower, lo, preferred_element_type=F32)
         + carry_ref[...])
    cc = jnp.concatenate(_split_bf16(c * LOG2E, N_SPLIT), axis=1)
    xk_ref[...] = (jnp.dot(cc, pk_ref[...], preferred_element_type=F32) + ok_ref[...]).astype(BF16)
    yq_ref[...] = (jnp.dot(cc, pq_ref[...], preferred_element_type=F32) + oq_ref[...]).astype(BF16)
    carry_ref[...] = c[blk - 1:blk, :]
    last_ref[0] = c[blk - 1:blk, :]


def _fgate_cumsum(f_small, bias_row, init_row, lane_maps, n_batch, n_lead_pad=0, blk=512):
    rows = f_small.shape[0]
    s = rows // n_batch
    blk = min(blk, s)
    nb = s // blk
    pk, pq, ones_k, ones_q = lane_maps
    width = pk.shape[1]
    const = lambda a: pl.BlockSpec(a.shape, lambda b, j: (0, 0))
    return pl.pallas_call(
        functools.partial(_fgate_cumsum_kernel, blk=blk, n_lead_pad=n_lead_pad),
        grid=(n_batch, nb),
        in_specs=[pl.BlockSpec((blk, LANES), lambda b, j: (b * nb + j, 0)),
                  pl.BlockSpec((1, LANES), lambda b, j: (0, 0)),
                  pl.BlockSpec((1, LANES), lambda b, j: (0, 0)),
                  const(pk), const(pq), const(ones_k), const(ones_q)],
        out_specs=[pl.BlockSpec((blk, width), lambda b, j: (b * nb + j, 0)),
                   pl.BlockSpec((blk, width), lambda b, j: (b * nb + j, 0)),
                   pl.BlockSpec((1, 1, LANES), lambda b, j: (b, 0, 0))],
        out_shape=[jax.ShapeDtypeStruct((rows, width), BF16), jax.ShapeDtypeStruct((rows, width), BF16),
                   jax.ShapeDtypeStruct((n_batch, 1, LANES), F32)],
        scratch_shapes=[pltpu.VMEM((1, LANES), F32)],
        compiler_params=_cparams("parallel", "arbitrary"),
        name="fgate_cumsum",
    )(f_small, bias_row, init_row, pk, pq, ones_k, ones_q)


def _attn_kernel(q_ref, k_ref, v_ref, xk_ref, yq_ref, km_ref, vm_ref, xkm_ref, o_ref,
                 kx_ref, vt_ref, m_ref, acc_ref, *, tq, tk, n_q, n_meta_pad):
    lo_half = lax.broadcasted_iota(I32, (tq, LANES), 1) < HEAD_DIM
    krow = lax.broadcasted_iota(I32, (tk, tq), 0)
    qcol = lax.broadcasted_iota(I32, (tk, tq), 1)
    meta_valid = lax.broadcasted_iota(I32, (LANES, tq), 0) >= n_meta_pad
    top_half = lax.broadcasted_iota(I32, (LANES, tq), 0) < HEAD_DIM
    nt = (((1,), (1,)), ((), ()))

    def lo_mask(rows):
        return lax.broadcasted_iota(I32, (rows, LANES), 1) < HEAD_DIM

    def with_ones_t(vals, lo, first):
        ones = jnp.ones(vals.shape, F32)
        vf = vals.astype(F32)
        return (jnp.where(lo, vf, ones) if first else jnp.where(lo, ones, vf)).T.astype(BF16)

    k, v, xk = k_ref[...], v_ref[...], xk_ref[...]
    lo_seq = lo_mask(k.shape[0])
    kx_ref[0] = jnp.where(lo_seq, k, xk)
    kx_ref[1] = jnp.where(lo_seq, xk, k)
    vt_ref[0] = with_ones_t(v, lo_seq, True)
    vt_ref[1] = with_ones_t(v, lo_seq, False)
    km, vm, xkm = km_ref[...], vm_ref[...], xkm_ref[...]
    lo_meta = lo_mask(km.shape[0])
    kmx = (jnp.where(lo_meta, km, xkm), jnp.where(lo_meta, xkm, km))
    vmt = (with_ones_t(vm, lo_meta, True), with_ones_t(vm, lo_meta, False))

    def q_block(qi, carry):
        q0 = pl.multiple_of(qi * tq, tq)
        q = q_ref[pl.ds(q0, tq), :]
        yq = yq_ref[pl.ds(q0, tq), :]
        qx = (jnp.where(lo_half, q, yq), jnp.where(lo_half, yq, q))

        for h in range(2):
            t = jnp.where(meta_valid, lax.dot_general(kmx[h], qx[h], nt, preferred_element_type=F32), NEG)
            m0 = jnp.max(t, axis=0, keepdims=True)
            m_ref[h] = m0
            acc_ref[h] = jnp.dot(vmt[h], jnp.exp2(t - m0).astype(BF16), preferred_element_type=F32)

        def update(k0, mask):
            ts = [lax.dot_general(kx_ref[h, pl.ds(k0, tk), :], qx[h], nt, preferred_element_type=F32)
                  for h in range(2)]
            for h in range(2):
                t = ts[h] if mask is None else jnp.where(mask, ts[h], NEG)
                m_prev = m_ref[h]
                m_new = jnp.maximum(m_prev, jnp.max(t, axis=0, keepdims=True))
                p = jnp.exp2(t - m_new).astype(BF16)
                acc_ref[h] = (jnp.exp2(m_prev - m_new) * acc_ref[h]
                              + jnp.dot(vt_ref[h, :, pl.ds(k0, tk)], p, preferred_element_type=F32))
                m_ref[h] = m_new

        def k_block(j, c):
            update(pl.multiple_of(j * tk, tk), None)
            return c

        lax.fori_loop(0, qi * (tq // tk), k_block, 0)
        for d in range(tq // tk):
            update(q0 + d * tk, krow + d * tk <= qcol)

        a0, a1 = acc_ref[0], acc_ref[1]
        out_t = jnp.where(top_half, a0 / a0[HEAD_DIM:HEAD_DIM + 1, :], a1 / a1[0:1, :])
        o_ref[pl.ds(q0, tq), :] = out_t.T.astype(o_ref.dtype)
        return carry

    lax.fori_loop(0, n_q, q_block, 0)


def _attention(big, big_meta, xk, yq, xk_meta, n_batch, seq, n_pairs, col_q, col_k, col_v, tq=512, tk=512):
    tq = min(tq, seq)
    tk = min(tk, tq)
    n_q = seq // tq
    mrows = big_meta.shape[0]
    kern = functools.partial(_attn_kernel, tq=tq, tk=tk, n_q=n_q, n_meta_pad=mrows - N_META)
    blk = lambda c0: pl.BlockSpec((seq, LANES), lambda b, p, c0=c0: (b, c0 + p))
    mblk = lambda c0: pl.BlockSpec((mrows, LANES), lambda b, p, c0=c0: (0, c0 + p))
    return pl.pallas_call(
        kern,
        grid=(n_batch, n_pairs),
        in_specs=[blk(col_q), blk(col_k), blk(col_v), blk(0), blk(0), mblk(col_k), mblk(col_v), mblk(0)],
        out_specs=pl.BlockSpec((seq, LANES), lambda b, p: (b, p)),
        out_shape=jax.ShapeDtypeStruct((n_batch * seq, n_pairs * LANES), BF16),
        scratch_shapes=[pltpu.VMEM((2, seq, LANES), BF16), pltpu.VMEM((2, LANES, seq), BF16),
                        pltpu.VMEM((2, 1, tq), F32), pltpu.VMEM((2, LANES, tq), F32)],
        compiler_params=_cparams("parallel", "parallel"),
        name="fox_attention",
    )(big, big, big, xk, yq, big_meta, big_meta, xk_meta)


def _ssd_kernel(x_ref, b_ref, c_ref, z_ref, dt_ref, wx_ref, wb_ref, wc_ref, bx_ref, bb_ref, bc_ref,
                hp_ref, dsk_ref, hx0_ref, hb0_ref, hc0_ref, s0_ref,
                y_ref, sout_ref, hxo_ref, hbo_ref, hco_ref,
                xpad_ref, bpad_ref, cpad_ref, state_ref, *, n_pairs, n_lead_pad):
    c = pl.program_id(2)
    n_c = pl.num_programs(2)
    L = CHUNK

    @pl.when(c == 0)
    def _():
        xpad_ref[0:HALO, :] = hx0_ref[0].astype(F32)
        bpad_ref[0:HALO, :] = hb0_ref[0].astype(F32)
        cpad_ref[0:HALO, :] = hc0_ref[0].astype(F32)
        state_ref[...] = s0_ref[0, 0]

    def conv_silu(pad_ref, raw_ref, w_ref, bias_ref):
        pad_ref[HALO:HALO + L, :] = raw_ref[...].astype(F32)
        acc = bias_ref[...]
        for kk in range(CONV_WIDTH):
            off = HALO - (CONV_WIDTH - 1) + kk
            acc = acc + w_ref[kk:kk + 1, :] * pad_ref[off:off + L, :]
        return acc * _sigmoid(acc)

    xs = conv_silu(xpad_ref, x_ref, wx_ref, bx_ref)
    bm = conv_silu(bpad_ref, b_ref, wb_ref, bb_ref)
    cm = conv_silu(cpad_ref, c_ref, wc_ref, bc_ref)

    hp = hp_ref[0]
    dt = _softplus(dt_ref[...] + hp[0:1, :])
    if n_lead_pad:
        row = lax.broadcasted_iota(I32, dt.shape, 0) + c * L
        dt = jnp.where(row < n_lead_pad, 0.0, dt)
    a = dt * (-jnp.exp(hp[1:2, :]))

    ri = lax.broadcasted_iota(I32, (L, L), 0)
    ci = lax.broadcasted_iota(I32, (L, L), 1)
    tri = ri >= ci
    tri_bf = tri.astype(BF16)
    a_hi, a_lo = _split_bf16(a)
    a_cs = jnp.dot(tri_bf, a_hi, preferred_element_type=F32) + jnp.dot(tri_bf, a_lo, preferred_element_type=F32)
    a_cst = a_cs.T
    a_last = a_cs[L - 1:L, :]
    ea = jnp.exp(a_cs)
    wdec = jnp.exp(a_last - a_cs)
    etot = jnp.exp(a_last)

    cm_bf = cm.astype(BF16)
    bm_bf = bm.astype(BF16)
    cb = lax.dot_general(cm_bf, bm_bf, (((1,), (1,)), ((), ())), preferred_element_type=F32)
    y_off = jnp.dot(cm_bf, state_ref[...].astype(BF16), preferred_element_type=F32)

    lane = lax.broadcasted_iota(I32, (L, LANES), 1)
    lo_half = lane < HEAD_DIM
    lo_row = lo_half[0:1, :]
    xw_parts = []
    etot_parts = []
    for p in range(n_pairs):
        ha, hb = 2 * p, 2 * p + 1
        sl = slice(p * LANES, (p + 1) * LANES)
        pick = lambda arr: jnp.where(lo_half, arr[:, ha:ha + 1], arr[:, hb:hb + 1])
        xs_p = xs[:, sl]
        xdt = xs_p * pick(dt)
        xdt_bf = xdt.astype(BF16)
        y_heads = []
        for hh in (ha, hb):
            seg = a_cs[:, hh:hh + 1] - a_cst[hh:hh + 1, :]
            dec = jnp.where(tri, jnp.exp(seg), 0.0)
            y_heads.append(jnp.dot((cb * dec).astype(BF16), xdt_bf, preferred_element_type=F32))
        y = jnp.where(lo_half, y_heads[0], y_heads[1]) + y_off[:, sl] * pick(ea) + dsk_ref[:, sl] * xs_p
        zf = z_ref[:, sl].astype(F32)
        y_ref[:, sl] = (y * (zf * _sigmoid(zf))).astype(y_ref.dtype)
        xw_parts.append((xdt * pick(wdec)).astype(BF16))
        etot_parts.append(jnp.where(lo_row, etot[:, ha:ha + 1], etot[:, hb:hb + 1]))

    xw = xw_parts[0] if n_pairs == 1 else jnp.concatenate(xw_parts, axis=1)
    etot_x = etot_parts[0] if n_pairs == 1 else jnp.concatenate(etot_parts, axis=1)
    state_ref[...] = state_ref[...] * etot_x + jnp.dot(bm.T.astype(BF16), xw, preferred_element_type=F32)

    xpad_ref[0:HALO, :] = xpad_ref[L:L + HALO, :]
    bpad_ref[0:HALO, :] = bpad_ref[L:L + HALO, :]
    cpad_ref[0:HALO, :] = cpad_ref[L:L + HALO, :]

    @pl.when(c == n_c - 1)
    def _():
        sout_ref[0, 0] = state_ref[...]
        hxo_ref[0] = xpad_ref[0:HALO, :]
        hbo_ref[0] = bpad_ref[0:HALO, :]
        hco_ref[0] = cpad_ref[0:HALO, :]


def _ssd(big, dtg, conv_w, conv_b, head_params, dskip_x, halo_x, halo_b, halo_c, state0,
         n_batch, seq, d_inner, col_z, col_x, n_lead_pad=0):
    G, N, L = SSM_GROUPS, SSM_STATE, CHUNK
    pg = d_inner // G
    n_pairs = pg // LANES
    n_c = seq // L
    cx, cz = col_x // pg, col_z // pg
    cb0 = (col_x + d_inner) // N
    cc0 = cb0 + G
    row = lambda b, g, c: b * n_c + c
    kern = functools.partial(_ssd_kernel, n_pairs=n_pairs, n_lead_pad=n_lead_pad)
    in_specs = [
        pl.BlockSpec((L, pg), lambda b, g, c: (row(b, g, c), cx + g)),
        pl.BlockSpec((L, N), lambda b, g, c: (row(b, g, c), cb0 + g)),
        pl.BlockSpec((L, N), lambda b, g, c: (row(b, g, c), cc0 + g)),
        pl.BlockSpec((L, pg), lambda b, g, c: (row(b, g, c), cz + g)),
        pl.BlockSpec((L, LANES), lambda b, g, c: (row(b, g, c), g)),
        pl.BlockSpec((CONV_WIDTH, pg), lambda b, g, c: (0, g)),
        pl.BlockSpec((CONV_WIDTH, N), lambda b, g, c: (0, d_inner // N + g)),
        pl.BlockSpec((CONV_WIDTH, N), lambda b, g, c: (0, d_inner // N + G + g)),
        pl.BlockSpec((1, pg), lambda b, g, c: (0, g)),
        pl.BlockSpec((1, N), lambda b, g, c: (0, d_inner // N + g)),
        pl.BlockSpec((1, N), lambda b, g, c: (0, d_inner // N + G + g)),
        pl.BlockSpec((1, 8, LANES), lambda b, g, c: (g, 0, 0)),
        pl.BlockSpec((1, pg), lambda b, g, c: (0, g)),
        pl.BlockSpec((1, HALO, pg), lambda b, g, c: (0, 0, g)),
        pl.BlockSpec((1, HALO, N), lambda b, g, c: (0, 0, g)),
        pl.BlockSpec((1, HALO, N), lambda b, g, c: (0, 0, g)),
        pl.BlockSpec((1, 1, N, pg), lambda b, g, c: (0, g, 0, 0)),
    ]
    out_specs = [
        pl.BlockSpec((L, pg), lambda b, g, c: (row(b, g, c), g)),
        pl.BlockSpec((1, 1, N, pg), lambda b, g, c: (b, g, 0, 0)),
        pl.BlockSpec((1, HALO, pg), lambda b, g, c: (b, 0, g)),
        pl.BlockSpec((1, HALO, N), lambda b, g, c: (b, 0, g)),
        pl.BlockSpec((1, HALO, N), lambda b, g, c: (b, 0, g)),
    ]
    out_shape = [
        jax.ShapeDtypeStruct((n_batch * seq, d_inner), F32),
        jax.ShapeDtypeStruct((n_batch, G, N, pg), F32),
        jax.ShapeDtypeStruct((n_batch, HALO, d_inner), F32),
        jax.ShapeDtypeStruct((n_batch, HALO, G * N), F32),
        jax.ShapeDtypeStruct((n_batch, HALO, G * N), F32),
    ]
    return pl.pallas_call(
        kern,
        grid=(n_batch, G, n_c),
        in_specs=in_specs,
        out_specs=out_specs,
        out_shape=out_shape,
        scratch_shapes=[pltpu.VMEM((L + HALO, pg), F32), pltpu.VMEM((L + HALO, N), F32),
                        pltpu.VMEM((L + HALO, N), F32), pltpu.VMEM((N, pg), F32)],
        compiler_params=_cparams("parallel", "parallel", "arbitrary"),
        name="ssd_scan",
    )(big, big, big, big, dtg, conv_w, conv_w, conv_w, conv_b, conv_b, conv_b,
      head_params, dskip_x, halo_x, halo_b, halo_c, state0)


def _outproj_kernel(attn_ref, y_ref, x_ref, ga_ref, gs_ref, w_ref, gf_ref, wr_ref, br_ref,
                    h1_ref, n2_ref, logit_ref, *, attn_width):
    def norm(v, g):
        return v * lax.rsqrt(jnp.mean(v * v, axis=-1, keepdims=True) + EPS) * g

    an = norm(attn_ref[...].astype(F32), ga_ref[...]).astype(BF16)
    yn = norm(y_ref[...], gs_ref[...]).astype(BF16)
    mixed = (jnp.dot(an, w_ref[0:attn_width, :], preferred_element_type=F32)
             + jnp.dot(yn, w_ref[attn_width:, :], preferred_element_type=F32))
    h1 = x_ref[...] + mixed
    n2 = norm(h1, gf_ref[...])
    h1_ref[...] = h1
    half = n2.shape[1] // 2
    n2_ref[...] = _pack_bf16_pair(n2[:, :half], n2[:, half:])
    logit_ref[...] = jnp.dot(n2.astype(BF16), wr_ref[...], preferred_element_type=F32) + br_ref[...]


def _outproj(attn, yssm, x, g_attn, g_ssm, w_out, g_ffn, w_router, b_router, tm=512):
    m, d = x.shape
    aw, sw = attn.shape[1], yssm.shape[1]
    tm = min(tm, m)
    row = lambda w: pl.BlockSpec((tm, w), lambda i: (i, 0))
    full = lambda r, c: pl.BlockSpec((r, c), lambda i: (0, 0))
    return pl.pallas_call(
        functools.partial(_outproj_kernel, attn_width=aw),
        grid=(m // tm,),
        in_specs=[row(aw), row(sw), row(d), full(1, aw), full(1, sw), full(aw + sw, d), full(1, d),
                  full(d, LANES), full(1, LANES)],
        out_specs=[row(d), row(d // 2), row(LANES)],
        out_shape=[jax.ShapeDtypeStruct((m, d), F32), jax.ShapeDtypeStruct((m, d // 2), U32),
                   jax.ShapeDtypeStruct((m, LANES), F32)],
        compiler_params=_cparams("parallel"),
        name="outproj_router",
    )(attn, yssm, x, g_attn.reshape(1, aw), g_ssm.reshape(1, sw), w_out, g_ffn.reshape(1, d),
      w_router, b_router)


def _route_kernel(logit_ref, idx_ref, gate_ref, cnt_ref, carry_ref, *, tm):
    i = pl.program_id(0)

    @pl.when(i == 0)
    def _():
        carry_ref[...] = jnp.zeros_like(carry_ref)

    work = logit_ref[...]
    lane = lax.broadcasted_iota(I32, work.shape, 1)
    lane_f = lane.astype(F32)
    vals, ids, hots = [], [], []
    for _ in range(TOP_K):
        mk = jnp.max(work, axis=-1, keepdims=True)
        ik = jnp.min(jnp.where(work == mk, lane_f, float(LANES)), axis=-1, keepdims=True)
        hot = lane_f == ik
        work = jnp.where(hot, -jnp.inf, work)
        vals.append(mk)
        ids.append(ik)
        hots.append(hot)
    exps = [jnp.exp(v - vals[0]) for v in vals]
    denom = exps[0] + exps[1] + exps[2] + exps[3]
    sel = jnp.zeros(work.shape, F32)
    for hot in hots:
        sel = sel + hot.astype(F32)
    strict = (lax.broadcasted_iota(I32, (tm, tm), 1) < lax.broadcasted_iota(I32, (tm, tm), 0)).astype(BF16)
    before = jnp.dot(strict, sel.astype(BF16), preferred_element_type=F32) + carry_ref[...]
    idx_out = jnp.zeros(work.shape, I32)
    gate_out = jnp.zeros(work.shape, F32)
    for k in range(TOP_K):
        rank = jnp.sum(jnp.where(hots[k], before, 0.0), axis=-1, keepdims=True).astype(I32)
        idx_out = jnp.where(lane == k, ids[k].astype(I32), idx_out)
        idx_out = jnp.where(lane == TOP_K + k, rank, idx_out)
        gate_out = jnp.where(lane == k, exps[k] / denom, gate_out)
    idx_ref[...] = idx_out
    gate_ref[...] = gate_out
    carry_ref[...] = carry_ref[...] + jnp.sum(sel, axis=0, keepdims=True)
    cnt_ref[...] = carry_ref[...]


def _route(logits, tm=512):
    m = logits.shape[0]
    tm = min(tm, m)
    return pl.pallas_call(
        functools.partial(_route_kernel, tm=tm),
        grid=(m // tm,),
        in_specs=[pl.BlockSpec((tm, LANES), lambda i: (i, 0))],
        out_specs=[pl.BlockSpec((tm, LANES), lambda i: (i, 0)), pl.BlockSpec((tm, LANES), lambda i: (i, 0)),
                   pl.BlockSpec((1, LANES), lambda i: (0, 0))],
        out_shape=[jax.ShapeDtypeStruct((m, LANES), I32), jax.ShapeDtypeStruct((m, LANES), F32),
                   jax.ShapeDtypeStruct((1, LANES), F32)],
        scratch_shapes=[pltpu.VMEM((1, LANES), F32)],
        compiler_params=_cparams("arbitrary"),
        name="route_top4",
    )(logits)


def _dispatch_kernel(dest_ref, src_ref, init_ref, out_ref, sem, *, td):
    del init_ref

    def issue(j, c):
        for k in range(TOP_K):
            pltpu.make_async_copy(src_ref.at[pl.ds(j, 1)], out_ref.at[pl.ds(dest_ref[j * TOP_K + k], 1)],
                                  sem).start()
        return c

    lax.fori_loop(0, td, issue, 0, unroll=8)
    for _ in range(TOP_K):
        pltpu.make_async_copy(src_ref, out_ref.at[pl.ds(0, td)], sem).wait()


def _dispatch(dest_flat, src, n_rows_out, td=256):
    m, d = src.shape
    td = min(td, m)
    init = jnp.zeros((n_rows_out, d), src.dtype)
    return pl.pallas_call(
        functools.partial(_dispatch_kernel, td=td),
        grid=(m // td,),
        in_specs=[pl.BlockSpec((td * TOP_K,), lambda i: (i,), memory_space=pltpu.SMEM),
                  pl.BlockSpec((td, d), lambda i: (i, 0)), pl.BlockSpec(memory_space=pl.ANY)],
        out_specs=pl.BlockSpec(memory_space=pl.ANY),
        out_shape=jax.ShapeDtypeStruct((n_rows_out, d), src.dtype),
        scratch_shapes=[pltpu.SemaphoreType.DMA],
        input_output_aliases={2: 0},
        compiler_params=_cparams("arbitrary"),
        name="moe_dispatch",
    )(dest_flat, src, init)


def _expert_changed(te_ref):
    m = pl.program_id(1)
    return jnp.logical_or(m == 0, te_ref[m] != te_ref[jnp.maximum(m - 1, 0)])


def _gate_up_kernel(te_ref, nu_ref, x_ref, wg_ref, wu_ref, bg_ref, bu_ref, o_ref, wgb_ref, wub_ref):
    @pl.when(_expert_changed(te_ref))
    def _():
        wgb_ref[...] = wg_ref[0].astype(BF16)
        wub_ref[...] = wu_ref[0].astype(BF16)

    @pl.when(pl.program_id(1) < nu_ref[0])
    def _():
        lo, hi = _unpack_bf16_pair(x_ref[...])
        lo, hi = lo.astype(BF16), hi.astype(BF16)
        half = lo.shape[1]

        def proj(w_ref, b_ref):
            return (jnp.dot(lo, w_ref[0:half, :], preferred_element_type=F32)
                    + jnp.dot(hi, w_ref[half:, :], preferred_element_type=F32) + b_ref[0])

        g = proj(wgb_ref, bg_ref)
        u = proj(wub_ref, bu_ref)
        g = jnp.minimum(g, SWIGLU_LIMIT)
        u = jnp.clip(u, -SWIGLU_LIMIT, SWIGLU_LIMIT)
        o_ref[...] = (g * _sigmoid(SWIGLU_ALPHA * g) * (u + 1.0)).astype(o_ref.dtype)

    @pl.when(pl.program_id(1) >= nu_ref[0])
    def _():
        o_ref[...] = jnp.zeros_like(o_ref)


def _grouped_gate_up(tile_expert, n_used, xs, w_gu, b_gu, tm, tn=512):
    mp, dh = xs.shape
    d = 2 * dh
    ff = w_gu.shape[2] // 2
    tn = min(tn, ff)
    nt = ff // tn
    grid_spec = pltpu.PrefetchScalarGridSpec(
        num_scalar_prefetch=2,
        grid=(nt, mp // tm),
        in_specs=[pl.BlockSpec((tm, dh), lambda n, m, te, nu: (m, 0)),
                  pl.BlockSpec((1, d, tn), lambda n, m, te, nu: (te[m], 0, n)),
                  pl.BlockSpec((1, d, tn), lambda n, m, te, nu: (te[m], 0, nt + n)),
                  pl.BlockSpec((1, 1, tn), lambda n, m, te, nu: (te[m], 0, n)),
                  pl.BlockSpec((1, 1, tn), lambda n, m, te, nu: (te[m], 0, nt + n))],
        out_specs=pl.BlockSpec((tm, tn), lambda n, m, te, nu: (m, n)),
        scratch_shapes=[pltpu.VMEM((d, tn), BF16), pltpu.VMEM((d, tn), BF16)],
    )
    return pl.pallas_call(
        _gate_up_kernel,
        grid_spec=grid_spec,
        out_shape=jax.ShapeDtypeStruct((mp, ff), BF16),
        compiler_params=_cparams("arbitrary", "arbitrary"),
        name="moe_gate_up",
    )(tile_expert, n_used, xs, w_gu, w_gu, b_gu, b_gu)


def _down_kernel(te_ref, nu_ref, h_ref, w_ref, b_ref, o_ref, wb_ref):
    @pl.when(_expert_changed(te_ref))
    def _():
        wb_ref[...] = w_ref[0].astype(BF16)

    @pl.when(pl.program_id(1) < nu_ref[0])
    def _():
        y = jnp.dot(h_ref[...], wb_ref[...], preferred_element_type=F32) + b_ref[0]
        half = y.shape[1] // 2
        o_ref[...] = _pack_bf16_pair(y[:, :half], y[:, half:])

    @pl.when(pl.program_id(1) >= nu_ref[0])
    def _():
        zero = jnp.zeros(o_ref.shape, F32)
        o_ref[...] = _pack_bf16_pair(zero, zero)


def _grouped_down(tile_expert, n_used, hdn, w_d, b_d, tm, tn=1024):
    mp, ff = hdn.shape
    d = w_d.shape[2]
    tn = min(tn, d)
    grid_spec = pltpu.PrefetchScalarGridSpec(
        num_scalar_prefetch=2,
        grid=(d // tn, mp // tm),
        in_specs=[pl.BlockSpec((tm, ff), lambda n, m, te, nu: (m, 0)),
                  pl.BlockSpec((1, ff, tn), lambda n, m, te, nu: (te[m], 0, n)),
                  pl.BlockSpec((1, 1, tn), lambda n, m, te, nu: (te[m], 0, n))],
        out_specs=pl.BlockSpec((tm, tn // 2), lambda n, m, te, nu: (m, n)),
        scratch_shapes=[pltpu.VMEM((ff, tn), BF16)],
    )
    return pl.pallas_call(
        _down_kernel,
        grid_spec=grid_spec,
        out_shape=jax.ShapeDtypeStruct((mp, d // 2), U32),
        compiler_params=_cparams("arbitrary", "arbitrary"),
        name="moe_down",
    )(tile_expert, n_used, hdn, w_d, b_d)


def _combine_kernel(dest_ref, ys_ref, gate_ref, h1_ref, g_ref, o_ref, buf_ref, sem, *, tc, pack_cols):
    def issue(j, c):
        for k in range(TOP_K):
            pltpu.make_async_copy(ys_ref.at[pl.ds(dest_ref[j * TOP_K + k], 1)], buf_ref.at[k, pl.ds(j, 1)],
                                  sem).start()
        return c

    lax.fori_loop(0, tc, issue, 0, unroll=8)
    for k in range(TOP_K):
        pltpu.make_async_copy(ys_ref.at[pl.ds(0, tc)], buf_ref.at[k], sem).wait()

    gates = gate_ref[...]
    hw = pack_cols // 2
    cols = []
    for n in range(buf_ref.shape[2] // hw):
        lo_acc = h1_ref[:, n * pack_cols:n * pack_cols + hw]
        hi_acc = h1_ref[:, n * pack_cols + hw:(n + 1) * pack_cols]
        for k in range(TOP_K):
            lo, hi = _unpack_bf16_pair(buf_ref[k, :, n * hw:(n + 1) * hw])
            lo_acc = lo_acc + gates[:, k:k + 1] * lo
            hi_acc = hi_acc + gates[:, k:k + 1] * hi
        cols += [lo_acc, hi_acc]
    h2 = jnp.concatenate(cols, axis=1)
    o_ref[...] = h2 * lax.rsqrt(jnp.mean(h2 * h2, axis=-1, keepdims=True) + EPS) * g_ref[...]


def _combine(dest_flat, ys, gates, h1, g_final, pack_cols, tc=256):
    m, d = h1.shape
    tc = min(tc, m)
    return pl.pallas_call(
        functools.partial(_combine_kernel, tc=tc, pack_cols=pack_cols),
        grid=(m // tc,),
        in_specs=[pl.BlockSpec((tc * TOP_K,), lambda i: (i,), memory_space=pltpu.SMEM),
                  pl.BlockSpec(memory_space=pl.ANY),
                  pl.BlockSpec((tc, LANES), lambda i: (i, 0)),
                  pl.BlockSpec((tc, d), lambda i: (i, 0)),
                  pl.BlockSpec((1, d), lambda i: (0, 0))],
        out_specs=pl.BlockSpec((tc, d), lambda i: (i, 0)),
        out_shape=jax.ShapeDtypeStruct((m, d), F32),
        scratch_shapes=[pltpu.VMEM((TOP_K, tc, d // 2), U32), pltpu.SemaphoreType.DMA],
        compiler_params=_cparams("arbitrary"),
        name="moe_combine",
    )(dest_flat, ys, gates, h1, g_final.reshape(1, d))


def _lane_pad(v, width=LANES):
    return jnp.pad(v, ((0, 0), (0, width - v.shape[1])))


def kernel(x, meta_tokens, norm_mix_g, w_in, fgate_b, attn_norm_g, conv_w, conv_b, dt_bias, a_log, d_skip,
           ssm_norm_g, w_out, norm_ffn_g, w_router, b_router, w_gate_up, b_gate_up, w_down, b_down,
           final_norm_g):
    n_batch, seq, d = x.shape
    aw = (d // 128) * HEAD_DIM
    n_ah = aw // HEAD_DIM
    sw = d
    n_sh = sw // HEAD_DIM
    hpg = n_sh // SSM_GROUPS
    gn = SSM_GROUPS * SSM_STATE
    tokens = n_batch * seq
    lyr = 0

    o_f = 3 * aw
    o_z = o_f + n_ah
    o_xbc = o_z + sw
    o_dt = o_xbc + sw + 2 * gn
    wi = w_in[lyr]
    w_q = wi[:, :aw] * (LOG2E * HEAD_DIM ** -0.5)
    w_big = jnp.concatenate([w_q, wi[:, aw:o_f], wi[:, o_z:o_dt]], axis=1).astype(BF16)
    w_dt = wi[:, o_dt:o_dt + n_sh].reshape(d, SSM_GROUPS, hpg)
    w_dt = jnp.pad(w_dt, ((0, 0), (0, 0), (0, LANES - hpg))).reshape(d, SSM_GROUPS * LANES)
    w_small = jnp.concatenate([_lane_pad(wi[:, o_f:o_z]), w_dt], axis=1).astype(BF16)
    col_k, col_v, col_z, col_x = aw, 2 * aw, 3 * aw, 3 * aw + sw

    meta_rows = jnp.concatenate([jnp.zeros((CHUNK - N_META, d), F32), meta_tokens.astype(F32)], axis=0)
    g_mix = norm_mix_g[lyr]
    n_meta = _rmsnorm_rows(meta_rows, g_mix, CHUNK)
    n_real = _rmsnorm_rows(x.reshape(tokens, d), g_mix, 512)
    big_meta = _matmul(n_meta, w_big, BF16, CHUNK, 1024, "inproj_meta")
    small_meta = _matmul(n_meta, w_small, F32, CHUNK, 640, "inproj_small_meta")
    big = _matmul(n_real, w_big, BF16, 1024, 1024, "inproj")
    small = _matmul(n_real, w_small, F32, 1024, 640, "inproj_small")

    fb = _lane_pad(fgate_b[lyr].reshape(1, n_ah).astype(F32))
    lane_maps = _gate_lane_maps(n_ah)
    xk_meta, _, c_meta_last = _fgate_cumsum(small_meta[:, :LANES], fb, jnp.zeros((1, LANES), F32), lane_maps, 1,
                                            n_lead_pad=CHUNK - N_META)
    xk, yq, _ = _fgate_cumsum(small[:, :LANES], fb, c_meta_last[0], lane_maps, n_batch)
    attn = _attention(big, big_meta, xk, yq, xk_meta, n_batch, seq, n_ah // 2, 0, col_k // LANES,
                      col_v // LANES)

    head_params = jnp.zeros((SSM_GROUPS, 8, LANES), F32)
    head_params = head_params.at[:, 0, :hpg].set(dt_bias[lyr].reshape(SSM_GROUPS, hpg).astype(F32))
    head_params = head_params.at[:, 1, :hpg].set(a_log[lyr].reshape(SSM_GROUPS, hpg).astype(F32))
    dskip_x = jnp.repeat(d_skip[lyr].astype(F32), HEAD_DIM).reshape(1, sw)
    cw, cb = conv_w[lyr].astype(F32), conv_b[lyr].astype(F32).reshape(1, -1)
    dtg_meta, dtg = small_meta[:, LANES:], small[:, LANES:]
    pg = sw // SSM_GROUPS
    zeros_h = lambda w: jnp.zeros((1, HALO, w), F32)
    state0 = jnp.zeros((1, SSM_GROUPS, SSM_STATE, pg), F32)
    _, s_meta, hx, hb, hc = _ssd(big_meta, dtg_meta, cw, cb, head_params, dskip_x, zeros_h(sw), zeros_h(gn),
                                 zeros_h(gn), state0, 1, CHUNK, sw, col_z, col_x, n_lead_pad=CHUNK - N_META)
    yssm = _ssd(big, dtg, cw, cb, head_params, dskip_x, hx, hb, hc, s_meta, n_batch, seq, sw, col_z, col_x)[0]

    wr = _lane_pad(w_router[lyr]).astype(BF16)
    br = jnp.concatenate([b_router[lyr].astype(F32), jnp.full((LANES - N_EXPERTS,), NEG, F32)]).reshape(1, LANES)
    h1, n2, logits = _outproj(attn, yssm, x.reshape(tokens, d), attn_norm_g[lyr], ssm_norm_g[lyr],
                              w_out[lyr].astype(BF16), norm_ffn_g[lyr], wr, br)

    idx, gates, counts = _route(logits)
    tm = min(512, tokens)
    eid, rank = idx[:, :TOP_K], idx[:, TOP_K:2 * TOP_K]
    cnt = counts[0, :N_EXPERTS].astype(I32)
    padded = ((cnt + tm - 1) // tm) * tm
    ends = jnp.cumsum(padded)
    starts = ends - padded
    dest = (starts[eid] + rank).reshape(-1)
    n_tiles = (tokens * TOP_K + N_EXPERTS * (tm - 1)) // tm
    tile_row0 = jnp.arange(n_tiles, dtype=I32) * tm
    tile_expert = jnp.minimum(jnp.sum((ends[None, :] <= tile_row0[:, None]).astype(I32), axis=1), N_EXPERTS - 1)
    n_used = (ends[-1] // tm).astype(I32).reshape(1)

    xs_sorted = _dispatch(dest, n2, n_tiles * tm)
    hdn = _grouped_gate_up(tile_expert, n_used, xs_sorted, w_gate_up[lyr],
                           b_gate_up[lyr].astype(F32).reshape(N_EXPERTS, 1, -1), tm)
    tn_down = min(1024, d)
    ys = _grouped_down(tile_expert, n_used, hdn, w_down[lyr], b_down[lyr].astype(F32).reshape(N_EXPERTS, 1, -1), tm,
                       tn=tn_down)
    out = _combine(dest, ys, gates, h1, final_norm_g, tn_down)
    return out.reshape(n_batch, seq, d)
```

```python
import functools

import numpy as np
import jax
import jax.numpy as jnp
from jax import lax
from jax.experimental import pallas as pl
from jax.experimental.pallas import tpu as pltpu

F32 = jnp.float32
BF16 = jnp.bfloat16
I32 = jnp.int32
U32 = jnp.uint32

N_META = 16
CHUNK = 128
HEAD_DIM = 64
SSM_GROUPS = 4
SSM_STATE = 128
CONV_WIDTH = 4
N_EXPERTS = 32
TOP_K = 4
SWIGLU_LIMIT = 7.0
SWIGLU_ALPHA = 1.702
EPS = 1e-5
LANES = 128
HALO = 16
NEG = -1e30
LOG2E = 1.4426950408889634
N_SPLIT = 3
HEAD_COPIES = 4
VMEM_LIMIT_BYTES = 56 * 1024 * 1024


def _cparams(*sem):
    return pltpu.CompilerParams(dimension_semantics=sem, vmem_limit_bytes=VMEM_LIMIT_BYTES)


def _sigmoid(x):
    return 0.5 * jnp.tanh(0.5 * x) + 0.5


def _softplus(x):
    return jnp.maximum(x, 0.0) + jnp.log(1.0 + jnp.exp(-jnp.abs(x)))


def _pack_bf16_pair(lo, hi):
    return pltpu.pack_elementwise([lo, hi], packed_dtype=BF16)


def _unpack_bf16_pair(w):
    lo = pltpu.unpack_elementwise(w, index=0, packed_dtype=BF16, unpacked_dtype=F32)
    hi = pltpu.unpack_elementwise(w, index=1, packed_dtype=BF16, unpacked_dtype=F32)
    return lo, hi


def _split_bf16(x, n=2):
    parts = []
    for _ in range(n):
        p = x.astype(BF16)
        parts.append(p)
        x = x - p.astype(F32)
    return parts


def _rmsnorm_kernel(x_ref, g_ref, o_ref):
    x = x_ref[...]
    ms = jnp.mean(x * x, axis=-1, keepdims=True)
    o_ref[...] = (x * lax.rsqrt(ms + EPS) * g_ref[...]).astype(o_ref.dtype)


def _rmsnorm_rows(x, g, tm):
    m, d = x.shape
    tm = min(tm, m)
    return pl.pallas_call(
        _rmsnorm_kernel,
        grid=(m // tm,),
        in_specs=[pl.BlockSpec((tm, d), lambda i: (i, 0)), pl.BlockSpec((1, d), lambda i: (0, 0))],
        out_specs=pl.BlockSpec((tm, d), lambda i: (i, 0)),
        out_shape=jax.ShapeDtypeStruct((m, d), BF16),
        compiler_params=_cparams("parallel"),
        name="rmsnorm_rows",
    )(x, g.reshape(1, d))


def _matmul_kernel(a_ref, w_ref, o_ref):
    o_ref[...] = jnp.dot(a_ref[...], w_ref[...], preferred_element_type=F32).astype(o_ref.dtype)


def _pick_tile(n, pref, unit=LANES):
    best = unit
    for t in range(unit, min(pref, n) + 1, unit):
        if n % t == 0:
            best = t
    return best


def _matmul(a, w, out_dtype, tm, tn, name):
    m, k = a.shape
    n = w.shape[1]
    tm, tn = min(tm, m), _pick_tile(n, tn)
    return pl.pallas_call(
        _matmul_kernel,
        grid=(n // tn, m // tm),
        in_specs=[pl.BlockSpec((tm, k), lambda j, i: (i, 0)), pl.BlockSpec((k, tn), lambda j, i: (0, j))],
        out_specs=pl.BlockSpec((tm, tn), lambda j, i: (i, j)),
        out_shape=jax.ShapeDtypeStruct((m, n), out_dtype),
        compiler_params=_cparams("parallel", "parallel"),
        name=name,
    )(a, w)


def _gate_lane_maps(n_heads):
    n_pairs = n_heads // 2
    pk = np.zeros((N_SPLIT * LANES, n_pairs * LANES), np.float32)
    pq = np.zeros_like(pk)
    ones_k = np.zeros((1, n_pairs * LANES), np.float32)
    ones_q = np.zeros_like(ones_k)
    for h in range(n_heads):
        base = (h // 2) * LANES + (HEAD_DIM if h % 2 == 0 else 0)
        for i in range(N_SPLIT):
            pk[i * LANES + h, base + i] = -1.0
            pq[i * LANES + h, base + N_SPLIT + i] = 1.0
            ones_q[0, base + i] = 1.0
            ones_k[0, base + N_SPLIT + i] = 1.0
    return (jnp.asarray(pk, BF16), jnp.asarray(pq, BF16), jnp.asarray(ones_k), jnp.asarray(ones_q))


def _fgate_cumsum_kernel(f_ref, b_ref, init_ref, pk_ref, pq_ref, ok_ref, oq_ref, xk_ref, yq_ref, last_ref,
                         carry_ref, *, blk, n_lead_pad):
    j = pl.program_id(1)

    @pl.when(j == 0)
    def _():
        carry_ref[...] = init_ref[...]

    x = f_ref[...] + b_ref[...]
    lf = jnp.minimum(x, 0.0) - jnp.log(1.0 + jnp.exp(-jnp.abs(x)))
    if n_lead_pad:
        row = lax.broadcasted_iota(I32, lf.shape, 0) + j * blk
        lf = jnp.where(row < n_lead_pad, 0.0, lf)
    hi, lo = _split_bf16(lf)
    lower = (lax.broadcasted_iota(I32, (blk, blk), 0) >= lax.broadcasted_iota(I32, (blk, blk), 1)).astype(BF16)
    c = (jnp.dot(lower, hi, preferred_element_type=F32) + jnp.dot(lower, lo, preferred_element_type=F32)
         + carry_ref[...])
    cc = jnp.concatenate(_split_bf16(c * LOG2E, N_SPLIT), axis=1)
    xk_ref[...] = (jnp.dot(cc, pk_ref[...], preferred_element_type=F32) + ok_ref[...]).astype(BF16)
    yq_ref[...] = (jnp.dot(cc, pq_ref[...], preferred_element_type=F32) + oq_ref[...]).astype(BF16)
    carry_ref[...] = c[blk - 1:blk, :]
    last_ref[0] = c[blk - 1:blk, :]


def _fgate_cumsum(f_small, bias_row, init_row, lane_maps, n_batch, n_lead_pad=0, blk=512):
    rows = f_small.shape[0]
    s = rows // n_batch
    blk = min(blk, s)
    nb = s // blk
    pk, pq, ones_k, ones_q = lane_maps
    width = pk.shape[1]
    const = lambda a: pl.BlockSpec(a.shape, lambda b, j: (0, 0))
    return pl.pallas_call(
        functools.partial(_fgate_cumsum_kernel, blk=blk, n_lead_pad=n_lead_pad),
        grid=(n_batch, nb),
        in_specs=[pl.BlockSpec((blk, LANES), lambda b, j: (b * nb + j, 0)),
                  pl.BlockSpec((1, LANES), lambda b, j: (0, 0)),
                  pl.BlockSpec((1, LANES), lambda b, j: (0, 0)),
                  const(pk), const(pq), const(ones_k), const(ones_q)],
        out_specs=[pl.BlockSpec((blk, width), lambda b, j: (b * nb + j, 0)),
                   pl.BlockSpec((blk, width), lambda b, j: (b * nb + j, 0)),
                   pl.BlockSpec((1, 1, LANES), lambda b, j: (b, 0, 0))],
        out_shape=[jax.ShapeDtypeStruct((rows, width), BF16), jax.ShapeDtypeStruct((rows, width), BF16),
                   jax.ShapeDtypeStruct((n_batch, 1, LANES), F32)],
        scratch_shapes=[pltpu.VMEM((1, LANES), F32)],
        compiler_params=_cparams("parallel", "arbitrary"),
        name="fgate_cumsum",
    )(f_small, bias_row, init_row, pk, pq, ones_k, ones_q)


def _attn_kernel(q_ref, k_ref, v_ref, xk_ref, yq_ref, km_ref, vm_ref, xkm_ref, o_ref,
                 kx_ref, vt_ref, m_ref, acc_ref, sa_ref, sb_ref, *, tq, tk, n_q, n_meta_pad):
    lo_half = lax.broadcasted_iota(I32, (tq, LANES), 1) < HEAD_DIM
    krow = lax.broadcasted_iota(I32, (tk, tq), 0)
    qcol = lax.broadcasted_iota(I32, (tk, tq), 1)
    meta_valid = lax.broadcasted_iota(I32, (LANES, tq), 0) >= n_meta_pad
    top_half = lax.broadcasted_iota(I32, (LANES, tq), 0) < HEAD_DIM
    nt = (((1,), (1,)), ((), ()))

    def lo_mask(rows):
        return lax.broadcasted_iota(I32, (rows, LANES), 1) < HEAD_DIM

    def with_ones_t(vals, lo, first):
        ones = jnp.ones(vals.shape, F32)
        vf = vals.astype(F32)
        return (jnp.where(lo, vf, ones) if first else jnp.where(lo, ones, vf)).T.astype(BF16)

    k, v, xk = k_ref[...], v_ref[...], xk_ref[...]
    lo_seq = lo_mask(k.shape[0])
    kx_ref[0] = jnp.where(lo_seq, k, xk)
    kx_ref[1] = jnp.where(lo_seq, xk, k)
    vt_ref[0] = with_ones_t(v, lo_seq, True)
    vt_ref[1] = with_ones_t(v, lo_seq, False)
    km, vm, xkm = km_ref[...], vm_ref[...], xkm_ref[...]
    lo_meta = lo_mask(km.shape[0])
    kmx = (jnp.where(lo_meta, km, xkm), jnp.where(lo_meta, xkm, km))
    vmt = (with_ones_t(vm, lo_meta, True), with_ones_t(vm, lo_meta, False))

    def q_block(qi, carry):
        q0 = pl.multiple_of(qi * tq, tq)
        q = q_ref[pl.ds(q0, tq), :]
        yq = yq_ref[pl.ds(q0, tq), :]
        qx = (jnp.where(lo_half, q, yq), jnp.where(lo_half, yq, q))

        for h in range(2):
            t = jnp.where(meta_valid, lax.dot_general(kmx[h], qx[h], nt, preferred_element_type=F32), NEG)
            m0 = jnp.max(t, axis=0, keepdims=True)
            m_ref[h] = m0
            acc_ref[h] = jnp.dot(vmt[h], jnp.exp2(t - m0).astype(BF16), preferred_element_type=F32)

        def scores(dst_ref, k0):
            for h in range(2):
                dst_ref[h] = lax.dot_general(kx_ref[h, pl.ds(k0, tk), :], qx[h], nt, preferred_element_type=F32)

        def softmax_pv(src_ref, k0, mask):
            for h in range(2):
                t = src_ref[h] if mask is None else jnp.where(mask, src_ref[h], NEG)
                m_prev = m_ref[h]
                m_new = jnp.maximum(m_prev, jnp.max(t, axis=0, keepdims=True))
                p = jnp.exp2(t - m_new).astype(BF16)
                acc_ref[h] = (jnp.exp2(m_prev - m_new) * acc_ref[h]
                              + jnp.dot(vt_ref[h, :, pl.ds(k0, tk)], p, preferred_element_type=F32))
                m_ref[h] = m_new

        block = lambda b: pl.multiple_of(b * tk, tk)
        causal = krow <= qcol
        scores(sa_ref, block(0))

        def block_pair(p, c):
            scores(sb_ref, block(2 * p + 1))
            softmax_pv(sa_ref, block(2 * p), None)
            scores(sa_ref, block(2 * p + 2))
            softmax_pv(sb_ref, block(2 * p + 1), None)
            return c

        lax.fori_loop(0, qi // 2, block_pair, 0)

        @pl.when(qi % 2 == 1)
        def _():
            scores(sb_ref, q0)
            softmax_pv(sa_ref, block(qi - 1), None)
            softmax_pv(sb_ref, q0, causal)

        @pl.when(qi % 2 == 0)
        def _():
            softmax_pv(sa_ref, q0, causal)

        a0, a1 = acc_ref[0], acc_ref[1]
        out_t = jnp.where(top_half, a0 / a0[HEAD_DIM:HEAD_DIM + 1, :], a1 / a1[0:1, :])
        o_ref[pl.ds(q0, tq), :] = out_t.T.astype(o_ref.dtype)
        return carry

    lax.fori_loop(0, n_q, q_block, 0)


def _attention(big, big_meta, xk, yq, xk_meta, n_batch, seq, n_pairs, col_q, col_k, col_v, tq=512, tk=512):
    tq = min(tq, seq)
    tk = min(tk, tq)
    n_q = seq // tq
    mrows = big_meta.shape[0]
    assert tk == tq, "the causal block is a single (tk, tq) block"
    kern = functools.partial(_attn_kernel, tq=tq, tk=tk, n_q=n_q, n_meta_pad=mrows - N_META)
    blk = lambda c0: pl.BlockSpec((seq, LANES), lambda b, p, c0=c0: (b, c0 + p))
    mblk = lambda c0: pl.BlockSpec((mrows, LANES), lambda b, p, c0=c0: (0, c0 + p))
    return pl.pallas_call(
        kern,
        grid=(n_batch, n_pairs),
        in_specs=[blk(col_q), blk(col_k), blk(col_v), blk(0), blk(0), mblk(col_k), mblk(col_v), mblk(0)],
        out_specs=pl.BlockSpec((seq, LANES), lambda b, p: (b, p)),
        out_shape=jax.ShapeDtypeStruct((n_batch * seq, n_pairs * LANES), BF16),
        scratch_shapes=[pltpu.VMEM((2, seq, LANES), BF16), pltpu.VMEM((2, LANES, seq), BF16),
                        pltpu.VMEM((2, 1, tq), F32), pltpu.VMEM((2, LANES, tq), F32),
                        pltpu.VMEM((2, tk, tq), F32), pltpu.VMEM((2, tk, tq), F32)],
        compiler_params=_cparams("parallel", "parallel"),
        name="fox_attention",
    )(big, big, big, xk, yq, big_meta, big_meta, xk_meta)


def _ssd_kernel(x_ref, b_ref, c_ref, z_ref, dt_ref, wx_ref, wb_ref, wc_ref, bx_ref, bb_ref, bc_ref,
                hp_ref, dsk_ref, shift_ref, spread_ref, hx0_ref, hb0_ref, hc0_ref, s0_ref,
                y_ref, sout_ref, hxo_ref, hbo_ref, hco_ref,
                xpad_ref, bpad_ref, cpad_ref, state_ref, *, n_pairs, n_sub, n_lead_pad):
    c = pl.program_id(2)
    n_c = pl.num_programs(2)
    L = CHUNK
    R = n_sub * L
    pg, hpg = n_pairs * LANES, 2 * n_pairs

    @pl.when(c == 0)
    def _():
        xpad_ref[0:HALO, :] = hx0_ref[0]
        bpad_ref[0:HALO, :] = hb0_ref[0]
        cpad_ref[0:HALO, :] = hc0_ref[0]
        state_ref[...] = s0_ref[0, 0]

    n_shift = CONV_WIDTH - 1
    xpad_ref[HALO:HALO + R, :] = x_ref[...]
    bpad_ref[HALO:HALO + R, :] = b_ref[...]
    cpad_ref[HALO:HALO + R, :] = c_ref[...]

    def conv_silu(pad_ref, w_ref, bias_ref, r0):
        taps = jnp.dot(shift_ref[...], pad_ref[r0:r0 + L + HALO, :], preferred_element_type=F32)
        acc = bias_ref[...] + w_ref[n_shift:n_shift + 1, :] * pad_ref[r0 + HALO:r0 + HALO + L, :].astype(F32)
        for kk in range(n_shift):
            acc = acc + w_ref[kk:kk + 1, :] * taps[kk * L:(kk + 1) * L, :]
        return acc * _sigmoid(acc)

    hp = hp_ref[0]
    neg_a = -jnp.exp(hp[1:2, :])
    ri = lax.broadcasted_iota(I32, (L, L), 0)
    ci = lax.broadcasted_iota(I32, (L, L), 1)
    tri = ri >= ci
    tri_bf = tri.astype(BF16)
    lane = lax.broadcasted_iota(I32, (L, LANES), 1)
    lo_half = lane < HEAD_DIM
    lo_row = lo_half[0:1, :]

    state = state_ref[...]
    for i in range(n_sub):
        r0 = i * L
        xs = conv_silu(xpad_ref, wx_ref, bx_ref, r0)
        bm = conv_silu(bpad_ref, wb_ref, bb_ref, r0)
        cm = conv_silu(cpad_ref, wc_ref, bc_ref, r0)

        dt = _softplus(dt_ref[r0:r0 + L, :] + hp[0:1, :])
        if n_lead_pad:
            row = lax.broadcasted_iota(I32, dt.shape, 0) + (c * R + r0)
            dt = jnp.where(row < n_lead_pad, 0.0, dt)
        a = dt * neg_a
        a_hi, a_lo = _split_bf16(a)
        a_cs = (jnp.dot(tri_bf, a_hi, preferred_element_type=F32)
                + jnp.dot(tri_bf, a_lo, preferred_element_type=F32))
        a_cst = a_cs.T
        a_last = a_cs[L - 1:L, :]
        etot = jnp.exp(a_last)
        cols = jnp.where(lane < hpg, dt,
                         jnp.where(lane < 2 * hpg, jnp.exp(a_cs),
                                   jnp.where(lane < 3 * hpg, jnp.exp(a_last - a_cs), a_cs)))
        spread = jnp.dot(jnp.concatenate(_split_bf16(cols), axis=1), spread_ref[...], preferred_element_type=F32)
        dt_x, ea_x, wdec_x = spread[:, 0:pg], spread[:, pg:2 * pg], spread[:, 2 * pg:3 * pg]

        cm_bf = cm.astype(BF16)
        bm_bf = bm.astype(BF16)
        cb = lax.dot_general(cm_bf, bm_bf, (((1,), (1,)), ((), ())), preferred_element_type=F32)
        y_off = jnp.dot(cm_bf, state.astype(BF16), preferred_element_type=F32)

        xw_parts = []
        etot_parts = []
        for p in range(n_pairs):
            ha, hb = 2 * p, 2 * p + 1
            sl = slice(p * LANES, (p + 1) * LANES)
            xs_p = xs[:, sl]
            xdt = xs_p * dt_x[:, sl]
            xdt_bf = xdt.astype(BF16)
            y_heads = []
            for hh in (ha, hb):
                seg = spread[:, 3 * pg + hh * L:3 * pg + (hh + 1) * L] - a_cst[hh:hh + 1, :]
                dec = jnp.where(tri, jnp.exp(seg), 0.0)
                y_heads.append(jnp.dot((cb * dec).astype(BF16), xdt_bf, preferred_element_type=F32))
            y = jnp.where(lo_half, y_heads[0], y_heads[1]) + y_off[:, sl] * ea_x[:, sl] + dsk_ref[:, sl] * xs_p
            zf = z_ref[r0:r0 + L, sl].astype(F32)
            y_ref[r0:r0 + L, sl] = (y * (zf * _sigmoid(zf))).astype(y_ref.dtype)
            xw_parts.append((xdt * wdec_x[:, sl]).astype(BF16))
            etot_parts.append(jnp.where(lo_row, etot[:, ha:ha + 1], etot[:, hb:hb + 1]))

        xw = xw_parts[0] if n_pairs == 1 else jnp.concatenate(xw_parts, axis=1)
        etot_x = etot_parts[0] if n_pairs == 1 else jnp.concatenate(etot_parts, axis=1)
        state = state * etot_x + jnp.dot(bm.T.astype(BF16), xw, preferred_element_type=F32)
    state_ref[...] = state

    xpad_ref[0:HALO, :] = xpad_ref[R:R + HALO, :]
    bpad_ref[0:HALO, :] = bpad_ref[R:R + HALO, :]
    cpad_ref[0:HALO, :] = cpad_ref[R:R + HALO, :]

    @pl.when(c == n_c - 1)
    def _():
        sout_ref[0, 0] = state_ref[...]
        hxo_ref[0] = xpad_ref[0:HALO, :]
        hbo_ref[0] = bpad_ref[0:HALO, :]
        hco_ref[0] = cpad_ref[0:HALO, :]


def _ssd(big, dtg, conv_w, conv_b, head_params, dskip_x, halo_x, halo_b, halo_c, state0,
         n_batch, seq, d_inner, col_z, col_x, n_lead_pad=0):
    G, N, L = SSM_GROUPS, SSM_STATE, CHUNK
    pg = d_inner // G
    n_pairs = pg // LANES
    n_sub = 2 if seq % (2 * L) == 0 else 1
    R = n_sub * L
    n_c = seq // R
    cx, cz = col_x // pg, col_z // pg
    cb0 = (col_x + d_inner) // N
    cc0 = cb0 + G
    row = lambda b, g, c: b * n_c + c
    kern = functools.partial(_ssd_kernel, n_pairs=n_pairs, n_sub=n_sub, n_lead_pad=n_lead_pad)
    n_shift = CONV_WIDTH - 1
    shift_np = np.zeros((n_shift * L, L + HALO), np.float32)
    for kk in range(n_shift):
        shift_np[kk * L + np.arange(L), np.arange(L) + HALO - n_shift + kk] = 1.0
    shift = jnp.asarray(shift_np, BF16)
    hpg = 2 * n_pairs
    spread_np = np.zeros((2, LANES, 3 * pg + hpg * L), np.float32)
    for h in range(hpg):
        for k in range(3):
            spread_np[:, k * hpg + h, k * pg + h * HEAD_DIM:k * pg + (h + 1) * HEAD_DIM] = 1.0
        spread_np[:, 3 * hpg + h, 3 * pg + h * L:3 * pg + (h + 1) * L] = 1.0
    spread = jnp.asarray(spread_np.reshape(2 * LANES, -1), BF16)
    in_specs = [
        pl.BlockSpec((R, pg), lambda b, g, c: (row(b, g, c), cx + g)),
        pl.BlockSpec((R, N), lambda b, g, c: (row(b, g, c), cb0 + g)),
        pl.BlockSpec((R, N), lambda b, g, c: (row(b, g, c), cc0 + g)),
        pl.BlockSpec((R, pg), lambda b, g, c: (row(b, g, c), cz + g)),
        pl.BlockSpec((R, LANES), lambda b, g, c: (row(b, g, c), g)),
        pl.BlockSpec((CONV_WIDTH, pg), lambda b, g, c: (0, g)),
        pl.BlockSpec((CONV_WIDTH, N), lambda b, g, c: (0, d_inner // N + g)),
        pl.BlockSpec((CONV_WIDTH, N), lambda b, g, c: (0, d_inner // N + G + g)),
        pl.BlockSpec((1, pg), lambda b, g, c: (0, g)),
        pl.BlockSpec((1, N), lambda b, g, c: (0, d_inner // N + g)),
        pl.BlockSpec((1, N), lambda b, g, c: (0, d_inner // N + G + g)),
        pl.BlockSpec((1, 8, LANES), lambda b, g, c: (g, 0, 0)),
        pl.BlockSpec((1, pg), lambda b, g, c: (0, g)),
        pl.BlockSpec(shift.shape, lambda b, g, c: (0, 0)),
        pl.BlockSpec(spread.shape, lambda b, g, c: (0, 0)),
        pl.BlockSpec((1, HALO, pg), lambda b, g, c: (0, 0, g)),
        pl.BlockSpec((1, HALO, N), lambda b, g, c: (0, 0, g)),
        pl.BlockSpec((1, HALO, N), lambda b, g, c: (0, 0, g)),
        pl.BlockSpec((1, 1, N, pg), lambda b, g, c: (0, g, 0, 0)),
    ]
    out_specs = [
        pl.BlockSpec((R, pg), lambda b, g, c: (row(b, g, c), g)),
        pl.BlockSpec((1, 1, N, pg), lambda b, g, c: (b, g, 0, 0)),
        pl.BlockSpec((1, HALO, pg), lambda b, g, c: (b, 0, g)),
        pl.BlockSpec((1, HALO, N), lambda b, g, c: (b, 0, g)),
        pl.BlockSpec((1, HALO, N), lambda b, g, c: (b, 0, g)),
    ]
    out_shape = [
        jax.ShapeDtypeStruct((n_batch * seq, d_inner), F32),
        jax.ShapeDtypeStruct((n_batch, G, N, pg), F32),
        jax.ShapeDtypeStruct((n_batch, HALO, d_inner), BF16),
        jax.ShapeDtypeStruct((n_batch, HALO, G * N), BF16),
        jax.ShapeDtypeStruct((n_batch, HALO, G * N), BF16),
    ]
    return pl.pallas_call(
        kern,
        grid=(n_batch, G, n_c),
        in_specs=in_specs,
        out_specs=out_specs,
        out_shape=out_shape,
        scratch_shapes=[pltpu.VMEM((R + HALO, pg), BF16), pltpu.VMEM((R + HALO, N), BF16),
                        pltpu.VMEM((R + HALO, N), BF16), pltpu.VMEM((N, pg), F32)],
        compiler_params=_cparams("parallel", "parallel", "arbitrary"),
        name="ssd_scan",
    )(big, big, big, big, dtg, conv_w, conv_w, conv_w, conv_b, conv_b, conv_b,
      head_params, dskip_x, shift, spread, halo_x, halo_b, halo_c, state0)


def _outproj_kernel(attn_ref, y_ref, x_ref, ga_ref, gs_ref, w_ref, gf_ref, wr_ref, br_ref,
                    h1_ref, n2_ref, logit_ref, *, attn_width):
    def norm(v, g):
        return v * lax.rsqrt(jnp.mean(v * v, axis=-1, keepdims=True) + EPS) * g

    an = norm(attn_ref[...].astype(F32), ga_ref[...]).astype(BF16)
    yn = norm(y_ref[...], gs_ref[...]).astype(BF16)
    mixed = (jnp.dot(an, w_ref[0:attn_width, :], preferred_element_type=F32)
             + jnp.dot(yn, w_ref[attn_width:, :], preferred_element_type=F32))
    h1 = x_ref[...] + mixed
    n2 = norm(h1, gf_ref[...])
    h1_ref[...] = h1
    half = n2.shape[1] // 2
    n2_ref[...] = _pack_bf16_pair(n2[:, :half], n2[:, half:])
    logit_ref[...] = jnp.dot(n2.astype(BF16), wr_ref[...], preferred_element_type=F32) + br_ref[...]


def _outproj(attn, yssm, x, g_attn, g_ssm, w_out, g_ffn, w_router, b_router, tm=512):
    m, d = x.shape
    aw, sw = attn.shape[1], yssm.shape[1]
    tm = min(tm, m)
    row = lambda w: pl.BlockSpec((tm, w), lambda i: (i, 0))
    full = lambda r, c: pl.BlockSpec((r, c), lambda i: (0, 0))
    return pl.pallas_call(
        functools.partial(_outproj_kernel, attn_width=aw),
        grid=(m // tm,),
        in_specs=[row(aw), row(sw), row(d), full(1, aw), full(1, sw), full(aw + sw, d), full(1, d),
                  full(d, LANES), full(1, LANES)],
        out_specs=[row(d), row(d // 2), row(LANES)],
        out_shape=[jax.ShapeDtypeStruct((m, d), F32), jax.ShapeDtypeStruct((m, d // 2), U32),
                   jax.ShapeDtypeStruct((m, LANES), F32)],
        compiler_params=_cparams("parallel"),
        name="outproj_router",
    )(attn, yssm, x, g_attn.reshape(1, aw), g_ssm.reshape(1, sw), w_out, g_ffn.reshape(1, d),
      w_router, b_router)


def _route_kernel(logit_ref, idx_ref, gate_ref, cnt_ref, carry_ref, *, tm):
    i = pl.program_id(0)

    @pl.when(i == 0)
    def _():
        carry_ref[...] = jnp.zeros_like(carry_ref)

    work = logit_ref[...]
    lane = lax.broadcasted_iota(I32, work.shape, 1)
    lane_f = lane.astype(F32)
    vals, ids, hots = [], [], []
    for _ in range(TOP_K):
        mk = jnp.max(work, axis=-1, keepdims=True)
        ik = jnp.min(jnp.where(work == mk, lane_f, float(LANES)), axis=-1, keepdims=True)
        hot = lane_f == ik
        work = jnp.where(hot, -jnp.inf, work)
        vals.append(mk)
        ids.append(ik)
        hots.append(hot)
    exps = [jnp.exp(v - vals[0]) for v in vals]
    denom = exps[0] + exps[1] + exps[2] + exps[3]
    sel = jnp.zeros(work.shape, F32)
    for hot in hots:
        sel = sel + hot.astype(F32)
    strict = (lax.broadcasted_iota(I32, (tm, tm), 1) < lax.broadcasted_iota(I32, (tm, tm), 0)).astype(BF16)
    before = jnp.dot(strict, sel.astype(BF16), preferred_element_type=F32) + carry_ref[...]
    idx_out = jnp.zeros(work.shape, I32)
    gate_out = jnp.zeros(work.shape, F32)
    for k in range(TOP_K):
        rank = jnp.sum(jnp.where(hots[k], before, 0.0), axis=-1, keepdims=True).astype(I32)
        idx_out = jnp.where(lane == k, ids[k].astype(I32), idx_out)
        idx_out = jnp.where(lane == TOP_K + k, rank, idx_out)
        gate_out = jnp.where(lane == k, exps[k] / denom, gate_out)
    idx_ref[...] = idx_out
    gate_ref[...] = gate_out
    carry_ref[...] = carry_ref[...] + jnp.sum(sel, axis=0, keepdims=True)
    cnt_ref[...] = carry_ref[...]


def _route(logits, tm=512):
    m = logits.shape[0]
    tm = min(tm, m)
    return pl.pallas_call(
        functools.partial(_route_kernel, tm=tm),
        grid=(m // tm,),
        in_specs=[pl.BlockSpec((tm, LANES), lambda i: (i, 0))],
        out_specs=[pl.BlockSpec((tm, LANES), lambda i: (i, 0)), pl.BlockSpec((tm, LANES), lambda i: (i, 0)),
                   pl.BlockSpec((1, LANES), lambda i: (0, 0))],
        out_shape=[jax.ShapeDtypeStruct((m, LANES), I32), jax.ShapeDtypeStruct((m, LANES), F32),
                   jax.ShapeDtypeStruct((1, LANES), F32)],
        scratch_shapes=[pltpu.VMEM((1, LANES), F32)],
        compiler_params=_cparams("arbitrary"),
        name="route_top4",
    )(logits)


def _dispatch_kernel(dest_ref, src_ref, init_ref, out_ref, sem, *, td):
    del init_ref

    def issue(j, c):
        for k in range(TOP_K):
            pltpu.make_async_copy(src_ref.at[pl.ds(j, 1)], out_ref.at[pl.ds(dest_ref[j * TOP_K + k], 1)],
                                  sem).start(priority=k % 2)
        return c

    lax.fori_loop(0, td, issue, 0, unroll=8)
    for _ in range(TOP_K):
        pltpu.make_async_copy(src_ref, out_ref.at[pl.ds(0, td)], sem).wait()


def _dispatch(dest_flat, src, n_rows_out, td=512):
    m, d = src.shape
    td = min(td, m)
    init = jnp.zeros((n_rows_out, d), src.dtype)
    return pl.pallas_call(
        functools.partial(_dispatch_kernel, td=td),
        grid=(m // td,),
        in_specs=[pl.BlockSpec((td * TOP_K,), lambda i: (i,), memory_space=pltpu.SMEM),
                  pl.BlockSpec((td, d), lambda i: (i, 0)), pl.BlockSpec(memory_space=pl.ANY)],
        out_specs=pl.BlockSpec(memory_space=pl.ANY),
        out_shape=jax.ShapeDtypeStruct((n_rows_out, d), src.dtype),
        scratch_shapes=[pltpu.SemaphoreType.DMA],
        input_output_aliases={2: 0},
        compiler_params=_cparams("arbitrary"),
        name="moe_dispatch",
    )(dest_flat, src, init)


def _expert_changed(te_ref):
    m = pl.program_id(1)
    return jnp.logical_or(m == 0, te_ref[m] != te_ref[jnp.maximum(m - 1, 0)])


def _gate_up_kernel(te_ref, nu_ref, x_ref, wg_ref, wu_ref, bg_ref, bu_ref, o_ref, wgb_ref, wub_ref):
    @pl.when(_expert_changed(te_ref))
    def _():
        wgb_ref[...] = wg_ref[0].astype(BF16)
        wub_ref[...] = wu_ref[0].astype(BF16)

    @pl.when(pl.program_id(1) < nu_ref[0])
    def _():
        lo, hi = _unpack_bf16_pair(x_ref[...])
        lo, hi = lo.astype(BF16), hi.astype(BF16)
        half = lo.shape[1]

        def proj(w_ref, b_ref):
            return (jnp.dot(lo, w_ref[0:half, :], preferred_element_type=F32)
                    + jnp.dot(hi, w_ref[half:, :], preferred_element_type=F32) + b_ref[0])

        g = proj(wgb_ref, bg_ref)
        u = proj(wub_ref, bu_ref)
        g = jnp.minimum(g, SWIGLU_LIMIT)
        u = jnp.clip(u, -SWIGLU_LIMIT, SWIGLU_LIMIT)
        o_ref[...] = (g * _sigmoid(SWIGLU_ALPHA * g) * (u + 1.0)).astype(o_ref.dtype)

    @pl.when(pl.program_id(1) >= nu_ref[0])
    def _():
        o_ref[...] = jnp.zeros_like(o_ref)


def _grouped_gate_up(tile_expert, n_used, xs, w_gu, b_gu, tm, tn=512):
    mp, dh = xs.shape
    d = 2 * dh
    ff = w_gu.shape[2] // 2
    tn = min(tn, ff)
    nt = ff // tn
    grid_spec = pltpu.PrefetchScalarGridSpec(
        num_scalar_prefetch=2,
        grid=(nt, mp // tm),
        in_specs=[pl.BlockSpec((tm, dh), lambda n, m, te, nu: (m, 0)),
                  pl.BlockSpec((1, d, tn), lambda n, m, te, nu: (te[m], 0, n)),
                  pl.BlockSpec((1, d, tn), lambda n, m, te, nu: (te[m], 0, nt + n)),
                  pl.BlockSpec((1, 1, tn), lambda n, m, te, nu: (te[m], 0, n)),
                  pl.BlockSpec((1, 1, tn), lambda n, m, te, nu: (te[m], 0, nt + n))],
        out_specs=pl.BlockSpec((tm, tn), lambda n, m, te, nu: (m, n)),
        scratch_shapes=[pltpu.VMEM((d, tn), BF16), pltpu.VMEM((d, tn), BF16)],
    )
    return pl.pallas_call(
        _gate_up_kernel,
        grid_spec=grid_spec,
        out_shape=jax.ShapeDtypeStruct((mp, ff), BF16),
        compiler_params=_cparams("arbitrary", "arbitrary"),
        name="moe_gate_up",
    )(tile_expert, n_used, xs, w_gu, w_gu, b_gu, b_gu)


def _down_kernel(te_ref, nu_ref, h_ref, w_ref, b_ref, o_ref, wb_ref):
    @pl.when(_expert_changed(te_ref))
    def _():
        wb_ref[...] = w_ref[0].astype(BF16)

    @pl.when(pl.program_id(1) < nu_ref[0])
    def _():
        y = jnp.dot(h_ref[...], wb_ref[...], preferred_element_type=F32) + b_ref[0]
        half = y.shape[1] // 2
        o_ref[...] = _pack_bf16_pair(y[:, :half], y[:, half:])

    @pl.when(pl.program_id(1) >= nu_ref[0])
    def _():
        zero = jnp.zeros(o_ref.shape, F32)
        o_ref[...] = _pack_bf16_pair(zero, zero)


def _grouped_down(tile_expert, n_used, hdn, w_d, b_d, tm, tn=1024):
    mp, ff = hdn.shape
    d = w_d.shape[2]
    tn = min(tn, d)
    grid_spec = pltpu.PrefetchScalarGridSpec(
        num_scalar_prefetch=2,
        grid=(d // tn, mp // tm),
        in_specs=[pl.BlockSpec((tm, ff), lambda n, m, te, nu: (m, 0)),
                  pl.BlockSpec((1, ff, tn), lambda n, m, te, nu: (te[m], 0, n)),
                  pl.BlockSpec((1, 1, tn), lambda n, m, te, nu: (te[m], 0, n))],
        out_specs=pl.BlockSpec((tm, tn // 2), lambda n, m, te, nu: (m, n)),
        scratch_shapes=[pltpu.VMEM((ff, tn), BF16)],
    )
    return pl.pallas_call(
        _down_kernel,
        grid_spec=grid_spec,
        out_shape=jax.ShapeDtypeStruct((mp, d // 2), U32),
        compiler_params=_cparams("arbitrary", "arbitrary"),
        name="moe_down",
    )(tile_expert, n_used, hdn, w_d, b_d)


def _combine_kernel(dest_ref, ys_ref, gate_ref, h1_ref, g_ref, o_ref, buf_ref, sem, *, tc, pack_cols):
    def issue(j, c):
        for k in range(TOP_K):
            pltpu.make_async_copy(ys_ref.at[pl.ds(dest_ref[j * TOP_K + k], 1)], buf_ref.at[k, pl.ds(j, 1)],
                                  sem).start(priority=k % 2)
        return c

    lax.fori_loop(0, tc, issue, 0, unroll=8)
    for k in range(TOP_K):
        pltpu.make_async_copy(ys_ref.at[pl.ds(0, tc)], buf_ref.at[k], sem).wait()

    gates = gate_ref[...]
    hw = pack_cols // 2
    cols = []
    for n in range(buf_ref.shape[2] // hw):
        lo_acc = h1_ref[:, n * pack_cols:n * pack_cols + hw]
        hi_acc = h1_ref[:, n * pack_cols + hw:(n + 1) * pack_cols]
        for k in range(TOP_K):
            lo, hi = _unpack_bf16_pair(buf_ref[k, :, n * hw:(n + 1) * hw])
            lo_acc = lo_acc + gates[:, k:k + 1] * lo
            hi_acc = hi_acc + gates[:, k:k + 1] * hi
        cols += [lo_acc, hi_acc]
    h2 = jnp.concatenate(cols, axis=1)
    o_ref[...] = h2 * lax.rsqrt(jnp.mean(h2 * h2, axis=-1, keepdims=True) + EPS) * g_ref[...]


def _combine(dest_flat, ys, gates, h1, g_final, pack_cols, tc=256):
    m, d = h1.shape
    tc = min(tc, m)
    return pl.pallas_call(
        functools.partial(_combine_kernel, tc=tc, pack_cols=pack_cols),
        grid=(m // tc,),
        in_specs=[pl.BlockSpec((tc * TOP_K,), lambda i: (i,), memory_space=pltpu.SMEM),
                  pl.BlockSpec(memory_space=pl.ANY),
                  pl.BlockSpec((tc, LANES), lambda i: (i, 0)),
                  pl.BlockSpec((tc, d), lambda i: (i, 0)),
                  pl.BlockSpec((1, d), lambda i: (0, 0))],
        out_specs=pl.BlockSpec((tc, d), lambda i: (i, 0)),
        out_shape=jax.ShapeDtypeStruct((m, d), F32),
        scratch_shapes=[pltpu.VMEM((TOP_K, tc, d // 2), U32), pltpu.SemaphoreType.DMA],
        compiler_params=_cparams("arbitrary"),
        name="moe_combine",
    )(dest_flat, ys, gates, h1, g_final.reshape(1, d))


def _lane_pad(v, width=LANES):
    return jnp.pad(v, ((0, 0), (0, width - v.shape[1])))


def kernel(x, meta_tokens, norm_mix_g, w_in, fgate_b, attn_norm_g, conv_w, conv_b, dt_bias, a_log, d_skip,
           ssm_norm_g, w_out, norm_ffn_g, w_router, b_router, w_gate_up, b_gate_up, w_down, b_down,
           final_norm_g):
    n_batch, seq, d = x.shape
    aw = (d // 128) * HEAD_DIM
    n_ah = aw // HEAD_DIM
    sw = d
    n_sh = sw // HEAD_DIM
    hpg = n_sh // SSM_GROUPS
    gn = SSM_GROUPS * SSM_STATE
    tokens = n_batch * seq
    lyr = 0

    o_f = 3 * aw
    o_z = o_f + n_ah
    o_xbc = o_z + sw
    o_dt = o_xbc + sw + 2 * gn
    wi = w_in[lyr]
    w_q = wi[:, :aw] * (LOG2E * HEAD_DIM ** -0.5)
    w_big = jnp.concatenate([w_q, wi[:, aw:o_f], wi[:, o_z:o_dt]], axis=1).astype(BF16)
    w_dt = jnp.tile(wi[:, o_dt:o_dt + n_sh].reshape(d, SSM_GROUPS, hpg), (1, 1, HEAD_COPIES))
    w_dt = jnp.pad(w_dt, ((0, 0), (0, 0), (0, LANES - HEAD_COPIES * hpg))).reshape(d, SSM_GROUPS * LANES)
    w_small = jnp.concatenate([_lane_pad(wi[:, o_f:o_z]), w_dt], axis=1).astype(BF16)
    col_k, col_v, col_z, col_x = aw, 2 * aw, 3 * aw, 3 * aw + sw

    meta_rows = jnp.concatenate([jnp.zeros((CHUNK - N_META, d), F32), meta_tokens.astype(F32)], axis=0)
    g_mix = norm_mix_g[lyr]
    n_meta = _rmsnorm_rows(meta_rows, g_mix, CHUNK)
    n_real = _rmsnorm_rows(x.reshape(tokens, d), g_mix, 512)
    big_meta = _matmul(n_meta, w_big, BF16, CHUNK, 1024, "inproj_meta")
    small_meta = _matmul(n_meta, w_small, F32, CHUNK, 640, "inproj_small_meta")
    big = _matmul(n_real, w_big, BF16, 1024, 1024, "inproj")
    small = _matmul(n_real, w_small, F32, 1024, 640, "inproj_small")

    fb = _lane_pad(fgate_b[lyr].reshape(1, n_ah).astype(F32))
    lane_maps = _gate_lane_maps(n_ah)
    xk_meta, _, c_meta_last = _fgate_cumsum(small_meta[:, :LANES], fb, jnp.zeros((1, LANES), F32), lane_maps, 1,
                                            n_lead_pad=CHUNK - N_META)
    xk, yq, _ = _fgate_cumsum(small[:, :LANES], fb, c_meta_last[0], lane_maps, n_batch)
    attn = _attention(big, big_meta, xk, yq, xk_meta, n_batch, seq, n_ah // 2, 0, col_k // LANES,
                      col_v // LANES)

    head_params = jnp.zeros((SSM_GROUPS, 8, LANES), F32)
    per_group = lambda v: jnp.tile(v.reshape(SSM_GROUPS, hpg).astype(F32), (1, HEAD_COPIES))
    head_params = head_params.at[:, 0, :HEAD_COPIES * hpg].set(per_group(dt_bias[lyr]))
    head_params = head_params.at[:, 1, :HEAD_COPIES * hpg].set(per_group(a_log[lyr]))
    dskip_x = jnp.repeat(d_skip[lyr].astype(F32), HEAD_DIM).reshape(1, sw)
    cw, cb = conv_w[lyr].astype(F32), conv_b[lyr].astype(F32).reshape(1, -1)
    dtg_meta, dtg = small_meta[:, LANES:], small[:, LANES:]
    pg = sw // SSM_GROUPS
    zeros_h = lambda w: jnp.zeros((1, HALO, w), BF16)
    state0 = jnp.zeros((1, SSM_GROUPS, SSM_STATE, pg), F32)
    _, s_meta, hx, hb, hc = _ssd(big_meta, dtg_meta, cw, cb, head_params, dskip_x, zeros_h(sw), zeros_h(gn),
                                 zeros_h(gn), state0, 1, CHUNK, sw, col_z, col_x, n_lead_pad=CHUNK - N_META)
    yssm = _ssd(big, dtg, cw, cb, head_params, dskip_x, hx, hb, hc, s_meta, n_batch, seq, sw, col_z, col_x)[0]

    wr = _lane_pad(w_router[lyr]).astype(BF16)
    br = jnp.concatenate([b_router[lyr].astype(F32), jnp.full((LANES - N_EXPERTS,), NEG, F32)]).reshape(1, LANES)
    h1, n2, logits = _outproj(attn, yssm, x.reshape(tokens, d), attn_norm_g[lyr], ssm_norm_g[lyr],
                              w_out[lyr].astype(BF16), norm_ffn_g[lyr], wr, br)

    idx, gates, counts = _route(logits)
    tm = min(512, tokens)
    eid, rank = idx[:, :TOP_K], idx[:, TOP_K:2 * TOP_K]
    cnt = counts[0, :N_EXPERTS].astype(I32)
    padded = ((cnt + tm - 1) // tm) * tm
    ends = jnp.cumsum(padded)
    starts = ends - padded
    dest = (starts[eid] + rank).reshape(-1)
    n_tiles = (tokens * TOP_K + N_EXPERTS * (tm - 1)) // tm
    tile_row0 = jnp.arange(n_tiles, dtype=I32) * tm
    tile_expert = jnp.minimum(jnp.sum((ends[None, :] <= tile_row0[:, None]).astype(I32), axis=1), N_EXPERTS - 1)
    n_used = (ends[-1] // tm).astype(I32).reshape(1)

    xs_sorted = _dispatch(dest, n2, n_tiles * tm)
    hdn = _grouped_gate_up(tile_expert, n_used, xs_sorted, w_gate_up[lyr],
                           b_gate_up[lyr].astype(F32).reshape(N_EXPERTS, 1, -1), tm)
    tn_down = min(1024, d)
    ys = _grouped_down(tile_expert, n_used, hdn, w_down[lyr], b_down[lyr].astype(F32).reshape(N_EXPERTS, 1, -1), tm,
                       tn=tn_down)
    out = _combine(dest, ys, gates, h1, final_norm_g, tn_down)
    return out.reshape(n_batch, seq, d)
```

```python
import functools

import numpy as np
import jax
import jax.numpy as jnp
from jax import lax
from jax.experimental import pallas as pl
from jax.experimental.pallas import tpu as pltpu

F32 = jnp.float32
BF16 = jnp.bfloat16
I32 = jnp.int32
U32 = jnp.uint32

N_META = 16
CHUNK = 128
HEAD_DIM = 64
SSM_GROUPS = 4
SSM_STATE = 128
CONV_WIDTH = 4
N_EXPERTS = 32
TOP_K = 4
SWIGLU_LIMIT = 7.0
SWIGLU_ALPHA = 1.702
EPS = 1e-5
LANES = 128
HALO = 16
NEG = -1e30
LOG2E = 1.4426950408889634
N_SPLIT = 3
HEAD_COPIES = 4
VMEM_LIMIT_BYTES = 56 * 1024 * 1024


def _cparams(*sem):
    return pltpu.CompilerParams(dimension_semantics=sem, vmem_limit_bytes=VMEM_LIMIT_BYTES)


def _sigmoid(x):
    return 0.5 * jnp.tanh(0.5 * x) + 0.5


def _softplus(x):
    return jnp.maximum(x, 0.0) + jnp.log(1.0 + jnp.exp(-jnp.abs(x)))


def _pack_bf16_pair(lo, hi):
    return pltpu.pack_elementwise([lo, hi], packed_dtype=BF16)


def _unpack_bf16_pair(w):
    lo = pltpu.unpack_elementwise(w, index=0, packed_dtype=BF16, unpacked_dtype=F32)
    hi = pltpu.unpack_elementwise(w, index=1, packed_dtype=BF16, unpacked_dtype=F32)
    return lo, hi


def _split_bf16(x, n=2):
    parts = []
    for _ in range(n):
        p = x.astype(BF16)
        parts.append(p)
        x = x - p.astype(F32)
    return parts


def _rmsnorm_kernel(x_ref, g_ref, o_ref):
    x = x_ref[...]
    ms = jnp.mean(x * x, axis=-1, keepdims=True)
    o_ref[...] = (x * lax.rsqrt(ms + EPS) * g_ref[...]).astype(o_ref.dtype)


def _rmsnorm_rows(x, g, tm):
    m, d = x.shape
    tm = min(tm, m)
    return pl.pallas_call(
        _rmsnorm_kernel,
        grid=(m // tm,),
        in_specs=[pl.BlockSpec((tm, d), lambda i: (i, 0)), pl.BlockSpec((1, d), lambda i: (0, 0))],
        out_specs=pl.BlockSpec((tm, d), lambda i: (i, 0)),
        out_shape=jax.ShapeDtypeStruct((m, d), BF16),
        compiler_params=_cparams("parallel"),
        name="rmsnorm_rows",
    )(x, g.reshape(1, d))


def _matmul_kernel(a_ref, wt_ref, o_ref):
    o_ref[...] = lax.dot_general(a_ref[...], wt_ref[...], (((1,), (1,)), ((), ())),
                                 preferred_element_type=F32).astype(o_ref.dtype)


def _pick_tile(n, pref, unit=LANES):
    best = unit
    for t in range(unit, min(pref, n) + 1, unit):
        if n % t == 0:
            best = t
    return best


def _matmul(a, wt, out_dtype, tm, tn, name):
    m, k = a.shape
    n = wt.shape[0]
    tm, tn = min(tm, m), _pick_tile(n, tn)
    return pl.pallas_call(
        _matmul_kernel,
        grid=(n // tn, m // tm),
        in_specs=[pl.BlockSpec((tm, k), lambda j, i: (i, 0)), pl.BlockSpec((tn, k), lambda j, i: (j, 0))],
        out_specs=pl.BlockSpec((tm, tn), lambda j, i: (i, j)),
        out_shape=jax.ShapeDtypeStruct((m, n), out_dtype),
        compiler_params=_cparams("parallel", "parallel"),
        name=name,
    )(a, wt)


def _gate_lane_maps(n_heads):
    n_pairs = n_heads // 2
    pk = np.zeros((N_SPLIT * LANES, n_pairs * LANES), np.float32)
    pq = np.zeros_like(pk)
    ones_k = np.zeros((1, n_pairs * LANES), np.float32)
    ones_q = np.zeros_like(ones_k)
    for h in range(n_heads):
        base = (h // 2) * LANES + (HEAD_DIM if h % 2 == 0 else 0)
        for i in range(N_SPLIT):
            pk[i * LANES + h, base + i] = -1.0
            pq[i * LANES + h, base + N_SPLIT + i] = 1.0
            ones_q[0, base + i] = 1.0
            ones_k[0, base + N_SPLIT + i] = 1.0
    return (jnp.asarray(pk, BF16), jnp.asarray(pq, BF16), jnp.asarray(ones_k), jnp.asarray(ones_q))


def _fgate_cumsum_kernel(f_ref, b_ref, init_ref, pk_ref, pq_ref, ok_ref, oq_ref, xk_ref, yq_ref, last_ref,
                         carry_ref, *, blk, n_lead_pad):
    j = pl.program_id(1)

    @pl.when(j == 0)
    def _():
        carry_ref[...] = init_ref[...]

    x = f_ref[...] + b_ref[...]
    lf = jnp.minimum(x, 0.0) - jnp.log(1.0 + jnp.exp(-jnp.abs(x)))
    if n_lead_pad:
        row = lax.broadcasted_iota(I32, lf.shape, 0) + j * blk
        lf = jnp.where(row < n_lead_pad, 0.0, lf)
    hi, lo = _split_bf16(lf)
    lower = (lax.broadcasted_iota(I32, (blk, blk), 0) >= lax.broadcasted_iota(I32, (blk, blk), 1)).astype(BF16)
    c = (jnp.dot(lower, hi, preferred_element_type=F32) + jnp.dot(lower, lo, preferred_element_type=F32)
         + carry_ref[...])
    cc = jnp.concatenate(_split_bf16(c * LOG2E, N_SPLIT), axis=1)
    xk_ref[...] = (jnp.dot(cc, pk_ref[...], preferred_element_type=F32) + ok_ref[...]).astype(BF16)
    yq_ref[...] = (jnp.dot(cc, pq_ref[...], preferred_element_type=F32) + oq_ref[...]).astype(BF16)
    carry_ref[...] = c[blk - 1:blk, :]
    last_ref[0] = c[blk - 1:blk, :]


def _fgate_cumsum(f_small, bias_row, init_row, lane_maps, n_batch, n_lead_pad=0, blk=512):
    rows = f_small.shape[0]
    s = rows // n_batch
    blk = min(blk, s)
    nb = s // blk
    pk, pq, ones_k, ones_q = lane_maps
    width = pk.shape[1]
    const = lambda a: pl.BlockSpec(a.shape, lambda b, j: (0, 0))
    return pl.pallas_call(
        functools.partial(_fgate_cumsum_kernel, blk=blk, n_lead_pad=n_lead_pad),
        grid=(n_batch, nb),
        in_specs=[pl.BlockSpec((blk, LANES), lambda b, j: (b * nb + j, 0)),
                  pl.BlockSpec((1, LANES), lambda b, j: (0, 0)),
                  pl.BlockSpec((1, LANES), lambda b, j: (0, 0)),
                  const(pk), const(pq), const(ones_k), const(ones_q)],
        out_specs=[pl.BlockSpec((blk, width), lambda b, j: (b * nb + j, 0)),
                   pl.BlockSpec((blk, width), lambda b, j: (b * nb + j, 0)),
                   pl.BlockSpec((1, 1, LANES), lambda b, j: (b, 0, 0))],
        out_shape=[jax.ShapeDtypeStruct((rows, width), BF16), jax.ShapeDtypeStruct((rows, width), BF16),
                   jax.ShapeDtypeStruct((n_batch, 1, LANES), F32)],
        scratch_shapes=[pltpu.VMEM((1, LANES), F32)],
        compiler_params=_cparams("parallel", "arbitrary"),
        name="fgate_cumsum",
    )(f_small, bias_row, init_row, pk, pq, ones_k, ones_q)


def _attn_kernel(q_ref, k_ref, v_ref, xk_ref, yq_ref, km_ref, vm_ref, xkm_ref, o_ref,
                 kx_ref, vt_ref, m_ref, acc_ref, sa_ref, sb_ref, *, tq, tk, n_q, n_meta_pad):
    lo_half = lax.broadcasted_iota(I32, (tq, LANES), 1) < HEAD_DIM
    krow = lax.broadcasted_iota(I32, (tk, tq), 0)
    qcol = lax.broadcasted_iota(I32, (tk, tq), 1)
    meta_valid = lax.broadcasted_iota(I32, (LANES, tq), 0) >= n_meta_pad
    top_half = lax.broadcasted_iota(I32, (LANES, tq), 0) < HEAD_DIM
    nt = (((1,), (1,)), ((), ()))

    def lo_mask(rows):
        return lax.broadcasted_iota(I32, (rows, LANES), 1) < HEAD_DIM

    def with_ones_t(vals, lo, first):
        ones = jnp.ones(vals.shape, F32)
        vf = vals.astype(F32)
        return (jnp.where(lo, vf, ones) if first else jnp.where(lo, ones, vf)).T.astype(BF16)

    k, v, xk = k_ref[...], v_ref[...], xk_ref[...]
    lo_seq = lo_mask(k.shape[0])
    kx_ref[0] = jnp.where(lo_seq, k, xk)
    kx_ref[1] = jnp.where(lo_seq, xk, k)
    vt_ref[0] = with_ones_t(v, lo_seq, True)
    vt_ref[1] = with_ones_t(v, lo_seq, False)
    km, vm, xkm = km_ref[...], vm_ref[...], xkm_ref[...]
    lo_meta = lo_mask(km.shape[0])
    kmx = (jnp.where(lo_meta, km, xkm), jnp.where(lo_meta, xkm, km))
    vmt = (with_ones_t(vm, lo_meta, True), with_ones_t(vm, lo_meta, False))

    def q_block(qi, carry):
        q0 = pl.multiple_of(qi * tq, tq)
        q = q_ref[pl.ds(q0, tq), :]
        yq = yq_ref[pl.ds(q0, tq), :]
        qx = (jnp.where(lo_half, q, yq), jnp.where(lo_half, yq, q))

        def scores(dst_ref, k0):
            for h in range(2):
                dst_ref[h] = lax.dot_general(kx_ref[h, pl.ds(k0, tk), :], qx[h], nt, preferred_element_type=F32)

        def softmax_pv(src_ref, k0, mask):
            for h in range(2):
                t = src_ref[h] if mask is None else jnp.where(mask, src_ref[h], NEG)
                m_prev = m_ref[h]
                m_new = jnp.maximum(m_prev, jnp.max(t, axis=0, keepdims=True))
                p = jnp.exp2(t - m_new).astype(BF16)
                acc_ref[h] = (jnp.exp2(m_prev - m_new) * acc_ref[h]
                              + jnp.dot(vt_ref[h, :, pl.ds(k0, tk)], p, preferred_element_type=F32))
                m_ref[h] = m_new

        block = lambda b: pl.multiple_of(b * tk, tk)
        causal = krow <= qcol
        scores(sa_ref, block(0))

        for h in range(2):
            t = jnp.where(meta_valid, lax.dot_general(kmx[h], qx[h], nt, preferred_element_type=F32), NEG)
            m0 = jnp.max(t, axis=0, keepdims=True)
            m_ref[h] = m0
            acc_ref[h] = jnp.dot(vmt[h], jnp.exp2(t - m0).astype(BF16), preferred_element_type=F32)

        def block_pair(p, c):
            scores(sb_ref, block(2 * p + 1))
            softmax_pv(sa_ref, block(2 * p), None)
            scores(sa_ref, block(2 * p + 2))
            softmax_pv(sb_ref, block(2 * p + 1), None)
            return c

        lax.fori_loop(0, qi // 2, block_pair, 0)

        @pl.when(qi % 2 == 1)
        def _():
            scores(sb_ref, q0)
            softmax_pv(sa_ref, block(qi - 1), None)
            softmax_pv(sb_ref, q0, causal)

        @pl.when(qi % 2 == 0)
        def _():
            softmax_pv(sa_ref, q0, causal)

        a0, a1 = acc_ref[0], acc_ref[1]
        out_t = jnp.where(top_half, a0 / a0[HEAD_DIM:HEAD_DIM + 1, :], a1 / a1[0:1, :])
        o_ref[pl.ds(q0, tq), :] = out_t.T.astype(o_ref.dtype)
        return carry

    lax.fori_loop(0, n_q, q_block, 0)


def _attention(big, big_meta, xk, yq, xk_meta, n_batch, seq, n_pairs, col_q, col_k, col_v, tq=512, tk=512):
    tq = min(tq, seq)
    tk = min(tk, tq)
    n_q = seq // tq
    mrows = big_meta.shape[0]
    assert tk == tq, "the causal block is a single (tk, tq) block"
    kern = functools.partial(_attn_kernel, tq=tq, tk=tk, n_q=n_q, n_meta_pad=mrows - N_META)
    blk = lambda c0: pl.BlockSpec((seq, LANES), lambda b, p, c0=c0: (b, c0 + p))
    mblk = lambda c0: pl.BlockSpec((mrows, LANES), lambda b, p, c0=c0: (0, c0 + p))
    return pl.pallas_call(
        kern,
        grid=(n_batch, n_pairs),
        in_specs=[blk(col_q), blk(col_k), blk(col_v), blk(0), blk(0), mblk(col_k), mblk(col_v), mblk(0)],
        out_specs=pl.BlockSpec((seq, LANES), lambda b, p: (b, p)),
        out_shape=jax.ShapeDtypeStruct((n_batch * seq, n_pairs * LANES), BF16),
        scratch_shapes=[pltpu.VMEM((2, seq, LANES), BF16), pltpu.VMEM((2, LANES, seq), BF16),
                        pltpu.VMEM((2, 1, tq), F32), pltpu.VMEM((2, LANES, tq), F32),
                        pltpu.VMEM((2, tk, tq), F32), pltpu.VMEM((2, tk, tq), F32)],
        compiler_params=_cparams("parallel", "parallel"),
        name="fox_attention",
    )(big, big, big, xk, yq, big_meta, big_meta, xk_meta)


def _ssd_kernel(x_ref, b_ref, c_ref, z_ref, dt_ref, wx_ref, wb_ref, wc_ref, bx_ref, bb_ref, bc_ref,
                hp_ref, dsk_ref, shift_ref, spread_ref, hx0_ref, hb0_ref, hc0_ref, s0_ref,
                y_ref, sout_ref, hxo_ref, hbo_ref, hco_ref,
                xpad_ref, bpad_ref, cpad_ref, state_ref, *, n_pairs, n_sub, n_lead_pad):
    c = pl.program_id(2)
    n_c = pl.num_programs(2)
    L = CHUNK
    R = n_sub * L
    pg, hpg = n_pairs * LANES, 2 * n_pairs

    @pl.when(c == 0)
    def _():
        xpad_ref[0:HALO, :] = hx0_ref[0]
        bpad_ref[0:HALO, :] = hb0_ref[0]
        cpad_ref[0:HALO, :] = hc0_ref[0]
        state_ref[...] = s0_ref[0, 0]

    n_shift = CONV_WIDTH - 1
    xpad_ref[HALO:HALO + R, :] = x_ref[...]
    bpad_ref[HALO:HALO + R, :] = b_ref[...]
    cpad_ref[HALO:HALO + R, :] = c_ref[...]

    def conv_silu(pad_ref, w_ref, bias_ref, r0):
        taps = jnp.dot(shift_ref[...], pad_ref[r0:r0 + L + HALO, :], preferred_element_type=F32)
        acc = bias_ref[...] + w_ref[n_shift:n_shift + 1, :] * pad_ref[r0 + HALO:r0 + HALO + L, :].astype(F32)
        for kk in range(n_shift):
            acc = acc + w_ref[kk:kk + 1, :] * taps[kk * L:(kk + 1) * L, :]
        return acc * _sigmoid(acc)

    hp = hp_ref[0]
    neg_a = -jnp.exp(hp[1:2, :])
    ri = lax.broadcasted_iota(I32, (L, L), 0)
    ci = lax.broadcasted_iota(I32, (L, L), 1)
    tri = ri >= ci
    tri_bf = tri.astype(BF16)
    lane = lax.broadcasted_iota(I32, (L, LANES), 1)
    lo_half = lane < HEAD_DIM
    lo_row = lo_half[0:1, :]

    state = state_ref[...]
    for i in range(n_sub):
        r0 = i * L
        xs = conv_silu(xpad_ref, wx_ref, bx_ref, r0)
        bm = conv_silu(bpad_ref, wb_ref, bb_ref, r0)
        cm = conv_silu(cpad_ref, wc_ref, bc_ref, r0)

        dt = _softplus(dt_ref[r0:r0 + L, :] + hp[0:1, :])
        if n_lead_pad:
            row = lax.broadcasted_iota(I32, dt.shape, 0) + (c * R + r0)
            dt = jnp.where(row < n_lead_pad, 0.0, dt)
        a = dt * neg_a
        a_hi, a_lo = _split_bf16(a)
        a_cs = (jnp.dot(tri_bf, a_hi, preferred_element_type=F32)
                + jnp.dot(tri_bf, a_lo, preferred_element_type=F32))
        a_cst = a_cs.T
        a_last = a_cs[L - 1:L, :]
        etot = jnp.exp(a_last)
        cols = jnp.where(lane < hpg, dt,
                         jnp.where(lane < 2 * hpg, jnp.exp(a_cs),
                                   jnp.where(lane < 3 * hpg, jnp.exp(a_last - a_cs), a_cs)))
        spread = jnp.dot(jnp.concatenate(_split_bf16(cols), axis=1), spread_ref[...], preferred_element_type=F32)
        dt_x, ea_x, wdec_x = spread[:, 0:pg], spread[:, pg:2 * pg], spread[:, 2 * pg:3 * pg]

        cm_bf = cm.astype(BF16)
        bm_bf = bm.astype(BF16)
        cb = lax.dot_general(cm_bf, bm_bf, (((1,), (1,)), ((), ())), preferred_element_type=F32)
        y_off = jnp.dot(cm_bf, state.astype(BF16), preferred_element_type=F32)

        xw_parts = []
        etot_parts = []
        for p in range(n_pairs):
            ha, hb = 2 * p, 2 * p + 1
            sl = slice(p * LANES, (p + 1) * LANES)
            xs_p = xs[:, sl]
            xdt = xs_p * dt_x[:, sl]
            xdt_bf = xdt.astype(BF16)
            y_heads = []
            for hh in (ha, hb):
                seg = spread[:, 3 * pg + hh * L:3 * pg + (hh + 1) * L] - a_cst[hh:hh + 1, :]
                dec = jnp.where(tri, jnp.exp(seg), 0.0)
                y_heads.append(jnp.dot((cb * dec).astype(BF16), xdt_bf, preferred_element_type=F32))
            y = jnp.where(lo_half, y_heads[0], y_heads[1]) + y_off[:, sl] * ea_x[:, sl] + dsk_ref[:, sl] * xs_p
            zf = z_ref[r0:r0 + L, sl].astype(F32)
            y_ref[r0:r0 + L, sl] = (y * (zf * _sigmoid(zf))).astype(y_ref.dtype)
            xw_parts.append((xdt * wdec_x[:, sl]).astype(BF16))
            etot_parts.append(jnp.where(lo_row, etot[:, ha:ha + 1], etot[:, hb:hb + 1]))

        xw = xw_parts[0] if n_pairs == 1 else jnp.concatenate(xw_parts, axis=1)
        etot_x = etot_parts[0] if n_pairs == 1 else jnp.concatenate(etot_parts, axis=1)
        state = state * etot_x + jnp.dot(bm.T.astype(BF16), xw, preferred_element_type=F32)
    state_ref[...] = state

    xpad_ref[0:HALO, :] = xpad_ref[R:R + HALO, :]
    bpad_ref[0:HALO, :] = bpad_ref[R:R + HALO, :]
    cpad_ref[0:HALO, :] = cpad_ref[R:R + HALO, :]

    @pl.when(c == n_c - 1)
    def _():
        sout_ref[0, 0] = state_ref[...]
        hxo_ref[0] = xpad_ref[0:HALO, :]
        hbo_ref[0] = bpad_ref[0:HALO, :]
        hco_ref[0] = cpad_ref[0:HALO, :]


def _ssd(big, dtg, conv_w, conv_b, head_params, dskip_x, halo_x, halo_b, halo_c, state0,
         n_batch, seq, d_inner, col_z, col_x, n_lead_pad=0):
    G, N, L = SSM_GROUPS, SSM_STATE, CHUNK
    pg = d_inner // G
    n_pairs = pg // LANES
    n_sub = 2 if seq % (2 * L) == 0 else 1
    R = n_sub * L
    n_c = seq // R
    cx, cz = col_x // pg, col_z // pg
    cb0 = (col_x + d_inner) // N
    cc0 = cb0 + G
    row = lambda b, g, c: b * n_c + c
    kern = functools.partial(_ssd_kernel, n_pairs=n_pairs, n_sub=n_sub, n_lead_pad=n_lead_pad)
    n_shift = CONV_WIDTH - 1
    shift_np = np.zeros((n_shift * L, L + HALO), np.float32)
    for kk in range(n_shift):
        shift_np[kk * L + np.arange(L), np.arange(L) + HALO - n_shift + kk] = 1.0
    shift = jnp.asarray(shift_np, BF16)
    hpg = 2 * n_pairs
    spread_np = np.zeros((2, LANES, 3 * pg + hpg * L), np.float32)
    for h in range(hpg):
        for k in range(3):
            spread_np[:, k * hpg + h, k * pg + h * HEAD_DIM:k * pg + (h + 1) * HEAD_DIM] = 1.0
        spread_np[:, 3 * hpg + h, 3 * pg + h * L:3 * pg + (h + 1) * L] = 1.0
    spread = jnp.asarray(spread_np.reshape(2 * LANES, -1), BF16)
    in_specs = [
        pl.BlockSpec((R, pg), lambda b, g, c: (row(b, g, c), cx + g)),
        pl.BlockSpec((R, N), lambda b, g, c: (row(b, g, c), cb0 + g)),
        pl.BlockSpec((R, N), lambda b, g, c: (row(b, g, c), cc0 + g)),
        pl.BlockSpec((R, pg), lambda b, g, c: (row(b, g, c), cz + g)),
        pl.BlockSpec((R, LANES), lambda b, g, c: (row(b, g, c), g)),
        pl.BlockSpec((CONV_WIDTH, pg), lambda b, g, c: (0, g)),
        pl.BlockSpec((CONV_WIDTH, N), lambda b, g, c: (0, d_inner // N + g)),
        pl.BlockSpec((CONV_WIDTH, N), lambda b, g, c: (0, d_inner // N + G + g)),
        pl.BlockSpec((1, pg), lambda b, g, c: (0, g)),
        pl.BlockSpec((1, N), lambda b, g, c: (0, d_inner // N + g)),
        pl.BlockSpec((1, N), lambda b, g, c: (0, d_inner // N + G + g)),
        pl.BlockSpec((1, 8, LANES), lambda b, g, c: (g, 0, 0)),
        pl.BlockSpec((1, pg), lambda b, g, c: (0, g)),
        pl.BlockSpec(shift.shape, lambda b, g, c: (0, 0)),
        pl.BlockSpec(spread.shape, lambda b, g, c: (0, 0)),
        pl.BlockSpec((1, HALO, pg), lambda b, g, c: (0, 0, g)),
        pl.BlockSpec((1, HALO, N), lambda b, g, c: (0, 0, g)),
        pl.BlockSpec((1, HALO, N), lambda b, g, c: (0, 0, g)),
        pl.BlockSpec((1, 1, N, pg), lambda b, g, c: (0, g, 0, 0)),
    ]
    out_specs = [
        pl.BlockSpec((R, pg), lambda b, g, c: (row(b, g, c), g)),
        pl.BlockSpec((1, 1, N, pg), lambda b, g, c: (b, g, 0, 0)),
        pl.BlockSpec((1, HALO, pg), lambda b, g, c: (b, 0, g)),
        pl.BlockSpec((1, HALO, N), lambda b, g, c: (b, 0, g)),
        pl.BlockSpec((1, HALO, N), lambda b, g, c: (b, 0, g)),
    ]
    out_shape = [
        jax.ShapeDtypeStruct((n_batch * seq, d_inner), F32),
        jax.ShapeDtypeStruct((n_batch, G, N, pg), F32),
        jax.ShapeDtypeStruct((n_batch, HALO, d_inner), BF16),
        jax.ShapeDtypeStruct((n_batch, HALO, G * N), BF16),
        jax.ShapeDtypeStruct((n_batch, HALO, G * N), BF16),
    ]
    return pl.pallas_call(
        kern,
        grid=(n_batch, G, n_c),
        in_specs=in_specs,
        out_specs=out_specs,
        out_shape=out_shape,
        scratch_shapes=[pltpu.VMEM((R + HALO, pg), BF16), pltpu.VMEM((R + HALO, N), BF16),
                        pltpu.VMEM((R + HALO, N), BF16), pltpu.VMEM((N, pg), F32)],
        compiler_params=_cparams("parallel", "parallel", "arbitrary"),
        name="ssd_scan",
    )(big, big, big, big, dtg, conv_w, conv_w, conv_w, conv_b, conv_b, conv_b,
      head_params, dskip_x, shift, spread, halo_x, halo_b, halo_c, state0)


def _outproj_kernel(attn_ref, y_ref, x_ref, ga_ref, gs_ref, w_ref, gf_ref, wr_ref, br_ref,
                    h1_ref, n2_ref, logit_ref, *, attn_width):
    def norm(v, g):
        return v * lax.rsqrt(jnp.mean(v * v, axis=-1, keepdims=True) + EPS) * g

    an = norm(attn_ref[...].astype(F32), ga_ref[...]).astype(BF16)
    yn = norm(y_ref[...], gs_ref[...]).astype(BF16)
    mixed = (jnp.dot(an, w_ref[0:attn_width, :], preferred_element_type=F32)
             + jnp.dot(yn, w_ref[attn_width:, :], preferred_element_type=F32))
    h1 = x_ref[...] + mixed
    n2 = norm(h1, gf_ref[...])
    h1_ref[...] = h1
    half = n2.shape[1] // 2
    n2_ref[...] = _pack_bf16_pair(n2[:, :half], n2[:, half:])
    logit_ref[...] = jnp.dot(n2.astype(BF16), wr_ref[...], preferred_element_type=F32) + br_ref[...]


def _outproj(attn, yssm, x, g_attn, g_ssm, w_out, g_ffn, w_router, b_router, tm=512):
    m, d = x.shape
    aw, sw = attn.shape[1], yssm.shape[1]
    tm = min(tm, m)
    row = lambda w: pl.BlockSpec((tm, w), lambda i: (i, 0))
    full = lambda r, c: pl.BlockSpec((r, c), lambda i: (0, 0))
    return pl.pallas_call(
        functools.partial(_outproj_kernel, attn_width=aw),
        grid=(m // tm,),
        in_specs=[row(aw), row(sw), row(d), full(1, aw), full(1, sw), full(aw + sw, d), full(1, d),
                  full(d, LANES), full(1, LANES)],
        out_specs=[row(d), row(d // 2), row(LANES)],
        out_shape=[jax.ShapeDtypeStruct((m, d), F32), jax.ShapeDtypeStruct((m, d // 2), U32),
                   jax.ShapeDtypeStruct((m, LANES), F32)],
        compiler_params=_cparams("parallel"),
        name="outproj_router",
    )(attn, yssm, x, g_attn.reshape(1, aw), g_ssm.reshape(1, sw), w_out, g_ffn.reshape(1, d),
      w_router, b_router)


def _route_kernel(logit_ref, idx_ref, gate_ref, cnt_ref, carry_ref, *, tm):
    i = pl.program_id(0)

    @pl.when(i == 0)
    def _():
        carry_ref[...] = jnp.zeros_like(carry_ref)

    work = logit_ref[...]
    lane = lax.broadcasted_iota(I32, work.shape, 1)
    lane_f = lane.astype(F32)
    vals, ids, hots = [], [], []
    for _ in range(TOP_K):
        mk = jnp.max(work, axis=-1, keepdims=True)
        ik = jnp.min(jnp.where(work == mk, lane_f, float(LANES)), axis=-1, keepdims=True)
        hot = lane_f == ik
        work = jnp.where(hot, -jnp.inf, work)
        vals.append(mk)
        ids.append(ik)
        hots.append(hot)
    exps = [jnp.exp(v - vals[0]) for v in vals]
    denom = exps[0] + exps[1] + exps[2] + exps[3]
    sel = jnp.zeros(work.shape, F32)
    for hot in hots:
        sel = sel + hot.astype(F32)
    strict = (lax.broadcasted_iota(I32, (tm, tm), 1) < lax.broadcasted_iota(I32, (tm, tm), 0)).astype(BF16)
    before = jnp.dot(strict, sel.astype(BF16), preferred_element_type=F32) + carry_ref[...]
    idx_out = jnp.zeros(work.shape, I32)
    gate_out = jnp.zeros(work.shape, F32)
    for k in range(TOP_K):
        rank = jnp.sum(jnp.where(hots[k], before, 0.0), axis=-1, keepdims=True).astype(I32)
        idx_out = jnp.where(lane == k, ids[k].astype(I32), idx_out)
        idx_out = jnp.where(lane == TOP_K + k, rank, idx_out)
        gate_out = jnp.where(lane == k, exps[k] / denom, gate_out)
    idx_ref[...] = idx_out
    gate_ref[...] = gate_out
    carry_ref[...] = carry_ref[...] + jnp.sum(sel, axis=0, keepdims=True)
    cnt_ref[...] = carry_ref[...]


def _route(logits, tm=512):
    m = logits.shape[0]
    tm = min(tm, m)
    return pl.pallas_call(
        functools.partial(_route_kernel, tm=tm),
        grid=(m // tm,),
        in_specs=[pl.BlockSpec((tm, LANES), lambda i: (i, 0))],
        out_specs=[pl.BlockSpec((tm, LANES), lambda i: (i, 0)), pl.BlockSpec((tm, LANES), lambda i: (i, 0)),
                   pl.BlockSpec((1, LANES), lambda i: (0, 0))],
        out_shape=[jax.ShapeDtypeStruct((m, LANES), I32), jax.ShapeDtypeStruct((m, LANES), F32),
                   jax.ShapeDtypeStruct((1, LANES), F32)],
        scratch_shapes=[pltpu.VMEM((1, LANES), F32)],
        compiler_params=_cparams("arbitrary"),
        name="route_top4",
    )(logits)


def _dispatch_kernel(dest_ref, src_ref, init_ref, out_ref, sem, *, td):
    del init_ref

    def issue(j, c):
        for k in range(TOP_K):
            pltpu.make_async_copy(src_ref.at[pl.ds(j, 1)], out_ref.at[pl.ds(dest_ref[j * TOP_K + k], 1)],
                                  sem).start(priority=k % 2)
        return c

    lax.fori_loop(0, td, issue, 0, unroll=8)
    for _ in range(TOP_K):
        pltpu.make_async_copy(src_ref, out_ref.at[pl.ds(0, td)], sem).wait()


def _dispatch(dest_flat, src, n_rows_out, td=512):
    m, d = src.shape
    td = min(td, m)
    init = jnp.zeros((n_rows_out, d), src.dtype)
    return pl.pallas_call(
        functools.partial(_dispatch_kernel, td=td),
        grid=(m // td,),
        in_specs=[pl.BlockSpec((td * TOP_K,), lambda i: (i,), memory_space=pltpu.SMEM),
                  pl.BlockSpec((td, d), lambda i: (i, 0)), pl.BlockSpec(memory_space=pl.ANY)],
        out_specs=pl.BlockSpec(memory_space=pl.ANY),
        out_shape=jax.ShapeDtypeStruct((n_rows_out, d), src.dtype),
        scratch_shapes=[pltpu.SemaphoreType.DMA],
        input_output_aliases={2: 0},
        compiler_params=_cparams("arbitrary"),
        name="moe_dispatch",
    )(dest_flat, src, init)


def _expert_changed(te_ref):
    m = pl.program_id(1)
    return jnp.logical_or(m == 0, te_ref[m] != te_ref[jnp.maximum(m - 1, 0)])


def _gate_up_kernel(te_ref, nu_ref, x_ref, wg_ref, wu_ref, bg_ref, bu_ref, o_ref, wgb_ref, wub_ref):
    @pl.when(_expert_changed(te_ref))
    def _():
        wgb_ref[...] = wg_ref[0].astype(BF16)
        wub_ref[...] = wu_ref[0].astype(BF16)

    @pl.when(pl.program_id(1) < nu_ref[0])
    def _():
        lo, hi = _unpack_bf16_pair(x_ref[...])
        lo, hi = lo.astype(BF16), hi.astype(BF16)
        half = lo.shape[1]

        def proj(w_ref, b_ref):
            return (jnp.dot(lo, w_ref[0:half, :], preferred_element_type=F32)
                    + jnp.dot(hi, w_ref[half:, :], preferred_element_type=F32) + b_ref[0])

        g = proj(wgb_ref, bg_ref)
        u = proj(wub_ref, bu_ref)
        g = jnp.minimum(g, SWIGLU_LIMIT)
        u = jnp.clip(u, -SWIGLU_LIMIT, SWIGLU_LIMIT)
        o_ref[...] = (g * _sigmoid(SWIGLU_ALPHA * g) * (u + 1.0)).astype(o_ref.dtype)

    @pl.when(pl.program_id(1) >= nu_ref[0])
    def _():
        o_ref[...] = jnp.zeros_like(o_ref)


def _grouped_gate_up(tile_expert, n_used, xs, w_gu, b_gu, tm, tn=512):
    mp, dh = xs.shape
    d = 2 * dh
    ff = w_gu.shape[2] // 2
    tn = min(tn, ff)
    nt = ff // tn
    grid_spec = pltpu.PrefetchScalarGridSpec(
        num_scalar_prefetch=2,
        grid=(nt, mp // tm),
        in_specs=[pl.BlockSpec((tm, dh), lambda n, m, te, nu: (m, 0)),
                  pl.BlockSpec((1, d, tn), lambda n, m, te, nu: (te[m], 0, n)),
                  pl.BlockSpec((1, d, tn), lambda n, m, te, nu: (te[m], 0, nt + n)),
                  pl.BlockSpec((1, 1, tn), lambda n, m, te, nu: (te[m], 0, n)),
                  pl.BlockSpec((1, 1, tn), lambda n, m, te, nu: (te[m], 0, nt + n))],
        out_specs=pl.BlockSpec((tm, tn), lambda n, m, te, nu: (m, n)),
        scratch_shapes=[pltpu.VMEM((d, tn), BF16), pltpu.VMEM((d, tn), BF16)],
    )
    return pl.pallas_call(
        _gate_up_kernel,
        grid_spec=grid_spec,
        out_shape=jax.ShapeDtypeStruct((mp, ff), BF16),
        compiler_params=_cparams("arbitrary", "arbitrary"),
        name="moe_gate_up",
    )(tile_expert, n_used, xs, w_gu, w_gu, b_gu, b_gu)


def _down_kernel(te_ref, nu_ref, h_ref, w_ref, b_ref, o_ref, wb_ref):
    @pl.when(_expert_changed(te_ref))
    def _():
        wb_ref[...] = w_ref[0].astype(BF16)

    @pl.when(pl.program_id(1) < nu_ref[0])
    def _():
        y = jnp.dot(h_ref[...], wb_ref[...], preferred_element_type=F32) + b_ref[0]
        half = y.shape[1] // 2
        o_ref[...] = _pack_bf16_pair(y[:, :half], y[:, half:])

    @pl.when(pl.program_id(1) >= nu_ref[0])
    def _():
        zero = jnp.zeros(o_ref.shape, F32)
        o_ref[...] = _pack_bf16_pair(zero, zero)


def _grouped_down(tile_expert, n_used, hdn, w_d, b_d, tm, tn=1024):
    mp, ff = hdn.shape
    d = w_d.shape[2]
    tn = min(tn, d)
    grid_spec = pltpu.PrefetchScalarGridSpec(
        num_scalar_prefetch=2,
        grid=(d // tn, mp // tm),
        in_specs=[pl.BlockSpec((tm, ff), lambda n, m, te, nu: (m, 0)),
                  pl.BlockSpec((1, ff, tn), lambda n, m, te, nu: (te[m], 0, n)),
                  pl.BlockSpec((1, 1, tn), lambda n, m, te, nu: (te[m], 0, n))],
        out_specs=pl.BlockSpec((tm, tn // 2), lambda n, m, te, nu: (m, n)),
        scratch_shapes=[pltpu.VMEM((ff, tn), BF16)],
    )
    return pl.pallas_call(
        _down_kernel,
        grid_spec=grid_spec,
        out_shape=jax.ShapeDtypeStruct((mp, d // 2), U32),
        compiler_params=_cparams("arbitrary", "arbitrary"),
        name="moe_down",
    )(tile_expert, n_used, hdn, w_d, b_d)


def _combine_kernel(dest_ref, ys_ref, gate_ref, h1_ref, g_ref, o_ref, buf_ref, sem, *, tc, pack_cols):
    def issue(j, c):
        for k in range(TOP_K):
            pltpu.make_async_copy(ys_ref.at[pl.ds(dest_ref[j * TOP_K + k], 1)], buf_ref.at[k, pl.ds(j, 1)],
                                  sem).start(priority=k % 2)
        return c

    lax.fori_loop(0, tc, issue, 0, unroll=8)
    for k in range(TOP_K):
        pltpu.make_async_copy(ys_ref.at[pl.ds(0, tc)], buf_ref.at[k], sem).wait()

    gates = gate_ref[...]
    hw = pack_cols // 2
    cols = []
    for n in range(buf_ref.shape[2] // hw):
        lo_acc = h1_ref[:, n * pack_cols:n * pack_cols + hw]
        hi_acc = h1_ref[:, n * pack_cols + hw:(n + 1) * pack_cols]
        for k in range(TOP_K):
            lo, hi = _unpack_bf16_pair(buf_ref[k, :, n * hw:(n + 1) * hw])
            lo_acc = lo_acc + gates[:, k:k + 1] * lo
            hi_acc = hi_acc + gates[:, k:k + 1] * hi
        cols += [lo_acc, hi_acc]
    h2 = jnp.concatenate(cols, axis=1)
    o_ref[...] = h2 * lax.rsqrt(jnp.mean(h2 * h2, axis=-1, keepdims=True) + EPS) * g_ref[...]


def _combine(dest_flat, ys, gates, h1, g_final, pack_cols, tc=512):
    m, d = h1.shape
    tc = min(tc, m)
    return pl.pallas_call(
        functools.partial(_combine_kernel, tc=tc, pack_cols=pack_cols),
        grid=(m // tc,),
        in_specs=[pl.BlockSpec((tc * TOP_K,), lambda i: (i,), memory_space=pltpu.SMEM),
                  pl.BlockSpec(memory_space=pl.ANY),
                  pl.BlockSpec((tc, LANES), lambda i: (i, 0)),
                  pl.BlockSpec((tc, d), lambda i: (i, 0)),
                  pl.BlockSpec((1, d), lambda i: (0, 0))],
        out_specs=pl.BlockSpec((tc, d), lambda i: (i, 0)),
        out_shape=jax.ShapeDtypeStruct((m, d), F32),
        scratch_shapes=[pltpu.VMEM((TOP_K, tc, d // 2), U32), pltpu.SemaphoreType.DMA],
        compiler_params=_cparams("arbitrary"),
        name="moe_combine",
    )(dest_flat, ys, gates, h1, g_final.reshape(1, d))


def _lane_pad(v, width=LANES):
    return jnp.pad(v, ((0, 0), (0, width - v.shape[1])))


def kernel(x, meta_tokens, norm_mix_g, w_in, fgate_b, attn_norm_g, conv_w, conv_b, dt_bias, a_log, d_skip,
           ssm_norm_g, w_out, norm_ffn_g, w_router, b_router, w_gate_up, b_gate_up, w_down, b_down,
           final_norm_g):
    n_batch, seq, d = x.shape
    aw = (d // 128) * HEAD_DIM
    n_ah = aw // HEAD_DIM
    sw = d
    n_sh = sw // HEAD_DIM
    hpg = n_sh // SSM_GROUPS
    gn = SSM_GROUPS * SSM_STATE
    tokens = n_batch * seq
    lyr = 0

    o_f = 3 * aw
    o_z = o_f + n_ah
    o_xbc = o_z + sw
    o_dt = o_xbc + sw + 2 * gn
    wt = jnp.swapaxes(w_in[lyr], 0, 1)
    w_q = wt[:aw] * (LOG2E * HEAD_DIM ** -0.5)
    w_big = jnp.concatenate([w_q, wt[aw:o_f], wt[o_z:o_dt]], axis=0).astype(BF16)
    w_dt = jnp.tile(wt[o_dt:o_dt + n_sh].reshape(SSM_GROUPS, hpg, d), (1, HEAD_COPIES, 1))
    w_dt = jnp.pad(w_dt, ((0, 0), (0, LANES - HEAD_COPIES * hpg), (0, 0))).reshape(SSM_GROUPS * LANES, d)
    w_f = jnp.pad(wt[o_f:o_z], ((0, LANES - n_ah), (0, 0)))
    w_small = jnp.concatenate([w_f, w_dt], axis=0).astype(BF16)
    col_k, col_v, col_z, col_x = aw, 2 * aw, 3 * aw, 3 * aw + sw

    meta_rows = jnp.concatenate([jnp.zeros((CHUNK - N_META, d), F32), meta_tokens.astype(F32)], axis=0)
    g_mix = norm_mix_g[lyr]
    n_meta = _rmsnorm_rows(meta_rows, g_mix, CHUNK)
    n_real = _rmsnorm_rows(x.reshape(tokens, d), g_mix, 512)
    big_meta = _matmul(n_meta, w_big, BF16, CHUNK, 1024, "inproj_meta")
    small_meta = _matmul(n_meta, w_small, F32, CHUNK, 640, "inproj_small_meta")
    big = _matmul(n_real, w_big, BF16, 1024, 1024, "inproj")
    small = _matmul(n_real, w_small, F32, 1024, 640, "inproj_small")

    fb = _lane_pad(fgate_b[lyr].reshape(1, n_ah).astype(F32))
    lane_maps = _gate_lane_maps(n_ah)
    xk_meta, _, c_meta_last = _fgate_cumsum(small_meta[:, :LANES], fb, jnp.zeros((1, LANES), F32), lane_maps, 1,
                                            n_lead_pad=CHUNK - N_META)
    xk, yq, _ = _fgate_cumsum(small[:, :LANES], fb, c_meta_last[0], lane_maps, n_batch)
    attn = _attention(big, big_meta, xk, yq, xk_meta, n_batch, seq, n_ah // 2, 0, col_k // LANES,
                      col_v // LANES)

    head_params = jnp.zeros((SSM_GROUPS, 8, LANES), F32)
    per_group = lambda v: jnp.tile(v.reshape(SSM_GROUPS, hpg).astype(F32), (1, HEAD_COPIES))
    head_params = head_params.at[:, 0, :HEAD_COPIES * hpg].set(per_group(dt_bias[lyr]))
    head_params = head_params.at[:, 1, :HEAD_COPIES * hpg].set(per_group(a_log[lyr]))
    dskip_x = jnp.repeat(d_skip[lyr].astype(F32), HEAD_DIM).reshape(1, sw)
    cw, cb = conv_w[lyr].astype(F32), conv_b[lyr].astype(F32).reshape(1, -1)
    dtg_meta, dtg = small_meta[:, LANES:], small[:, LANES:]
    pg = sw // SSM_GROUPS
    zeros_h = lambda w: jnp.zeros((1, HALO, w), BF16)
    state0 = jnp.zeros((1, SSM_GROUPS, SSM_STATE, pg), F32)
    _, s_meta, hx, hb, hc = _ssd(big_meta, dtg_meta, cw, cb, head_params, dskip_x, zeros_h(sw), zeros_h(gn),
                                 zeros_h(gn), state0, 1, CHUNK, sw, col_z, col_x, n_lead_pad=CHUNK - N_META)
    yssm = _ssd(big, dtg, cw, cb, head_params, dskip_x, hx, hb, hc, s_meta, n_batch, seq, sw, col_z, col_x)[0]

    wr = _lane_pad(w_router[lyr]).astype(BF16)
    br = jnp.concatenate([b_router[lyr].astype(F32), jnp.full((LANES - N_EXPERTS,), NEG, F32)]).reshape(1, LANES)
    h1, n2, logits = _outproj(attn, yssm, x.reshape(tokens, d), attn_norm_g[lyr], ssm_norm_g[lyr],
                              w_out[lyr].astype(BF16), norm_ffn_g[lyr], wr, br)

    idx, gates, counts = _route(logits)
    tm = min(512, tokens)
    eid, rank = idx[:, :TOP_K], idx[:, TOP_K:2 * TOP_K]
    cnt = counts[0, :N_EXPERTS].astype(I32)
    padded = ((cnt + tm - 1) // tm) * tm
    ends = jnp.cumsum(padded)
    starts = ends - padded
    dest = (starts[eid] + rank).reshape(-1)
    n_tiles = (tokens * TOP_K + N_EXPERTS * (tm - 1)) // tm
    tile_row0 = jnp.arange(n_tiles, dtype=I32) * tm
    tile_expert = jnp.minimum(jnp.sum((ends[None, :] <= tile_row0[:, None]).astype(I32), axis=1), N_EXPERTS - 1)
    n_used = (ends[-1] // tm).astype(I32).reshape(1)

    xs_sorted = _dispatch(dest, n2, n_tiles * tm)
    hdn = _grouped_gate_up(tile_expert, n_used, xs_sorted, w_gate_up[lyr],
                           b_gate_up[lyr].astype(F32).reshape(N_EXPERTS, 1, -1), tm)
    tn_down = min(1024, d)
    ys = _grouped_down(tile_expert, n_used, hdn, w_down[lyr], b_down[lyr].astype(F32).reshape(N_EXPERTS, 1, -1), tm,
                       tn=tn_down)
    out = _combine(dest, ys, gates, h1, final_norm_g, tn_down)
    return out.reshape(n_batch, seq, d)
```

```python
import functools

import numpy as np
import jax
import jax.numpy as jnp
from jax import lax
from jax.experimental import pallas as pl
from jax.experimental.pallas import tpu as pltpu

F32 = jnp.float32
BF16 = jnp.bfloat16
I32 = jnp.int32
U32 = jnp.uint32

N_META = 16
CHUNK = 128
HEAD_DIM = 64
SSM_GROUPS = 4
SSM_STATE = 128
CONV_WIDTH = 4
N_EXPERTS = 32
TOP_K = 4
SWIGLU_LIMIT = 7.0
SWIGLU_ALPHA = 1.702
EPS = 1e-5
LANES = 128
HALO = 16
NEG = -1e30
LOG2E = 1.4426950408889634
N_SPLIT = 3
HEAD_COPIES = 4
VMEM_LIMIT_BYTES = 56 * 1024 * 1024


def _cparams(*sem):
    return pltpu.CompilerParams(dimension_semantics=sem, vmem_limit_bytes=VMEM_LIMIT_BYTES)


def _sigmoid(x):
    return 0.5 * jnp.tanh(0.5 * x) + 0.5


def _softplus(x):
    return jnp.maximum(x, 0.0) + jnp.log(1.0 + jnp.exp(-jnp.abs(x)))


def _pack_bf16_pair(lo, hi):
    return pltpu.pack_elementwise([lo, hi], packed_dtype=BF16)


def _unpack_bf16_pair(w):
    lo = pltpu.unpack_elementwise(w, index=0, packed_dtype=BF16, unpacked_dtype=F32)
    hi = pltpu.unpack_elementwise(w, index=1, packed_dtype=BF16, unpacked_dtype=F32)
    return lo, hi


def _split_bf16(x, n=2):
    parts = []
    for _ in range(n):
        p = x.astype(BF16)
        parts.append(p)
        x = x - p.astype(F32)
    return parts


def _rmsnorm_kernel(x_ref, g_ref, o_ref):
    x = x_ref[...]
    ms = jnp.mean(x * x, axis=-1, keepdims=True)
    o_ref[...] = (x * lax.rsqrt(ms + EPS) * g_ref[...]).astype(o_ref.dtype)


def _rmsnorm_rows(x, g, tm):
    m, d = x.shape
    tm = min(tm, m)
    return pl.pallas_call(
        _rmsnorm_kernel,
        grid=(m // tm,),
        in_specs=[pl.BlockSpec((tm, d), lambda i: (i, 0)), pl.BlockSpec((1, d), lambda i: (0, 0))],
        out_specs=pl.BlockSpec((tm, d), lambda i: (i, 0)),
        out_shape=jax.ShapeDtypeStruct((m, d), BF16),
        compiler_params=_cparams("parallel"),
        name="rmsnorm_rows",
    )(x, g.reshape(1, d))


def _matmul_kernel(a_ref, wt_ref, o_ref):
    o_ref[...] = lax.dot_general(a_ref[...], wt_ref[...], (((1,), (1,)), ((), ())),
                                 preferred_element_type=F32).astype(o_ref.dtype)


def _pick_tile(n, pref, unit=LANES):
    best = unit
    for t in range(unit, min(pref, n) + 1, unit):
        if n % t == 0:
            best = t
    return best


def _matmul(a, wt, out_dtype, tm, tn, name):
    m, k = a.shape
    n = wt.shape[0]
    tm, tn = min(tm, m), _pick_tile(n, tn)
    return pl.pallas_call(
        _matmul_kernel,
        grid=(n // tn, m // tm),
        in_specs=[pl.BlockSpec((tm, k), lambda j, i: (i, 0)), pl.BlockSpec((tn, k), lambda j, i: (j, 0))],
        out_specs=pl.BlockSpec((tm, tn), lambda j, i: (i, j)),
        out_shape=jax.ShapeDtypeStruct((m, n), out_dtype),
        compiler_params=_cparams("parallel", "parallel"),
        name=name,
    )(a, wt)


def _gate_lane_maps(n_heads):
    n_pairs = n_heads // 2
    pk = np.zeros((N_SPLIT * LANES, n_pairs * LANES), np.float32)
    pq = np.zeros_like(pk)
    ones_k = np.zeros((1, n_pairs * LANES), np.float32)
    ones_q = np.zeros_like(ones_k)
    for h in range(n_heads):
        base = (h // 2) * LANES + (HEAD_DIM if h % 2 == 0 else 0)
        for i in range(N_SPLIT):
            pk[i * LANES + h, base + i] = -1.0
            pq[i * LANES + h, base + N_SPLIT + i] = 1.0
            ones_q[0, base + i] = 1.0
            ones_k[0, base + N_SPLIT + i] = 1.0
    return (jnp.asarray(pk, BF16), jnp.asarray(pq, BF16), jnp.asarray(ones_k), jnp.asarray(ones_q))


def _fgate_cumsum_kernel(f_ref, b_ref, init_ref, pk_ref, pq_ref, ok_ref, oq_ref, xk_ref, yq_ref, last_ref,
                         carry_ref, *, blk, n_lead_pad):
    j = pl.program_id(1)

    @pl.when(j == 0)
    def _():
        carry_ref[...] = init_ref[...]

    x = f_ref[...] + b_ref[...]
    lf = jnp.minimum(x, 0.0) - jnp.log(1.0 + jnp.exp(-jnp.abs(x)))
    if n_lead_pad:
        row = lax.broadcasted_iota(I32, lf.shape, 0) + j * blk
        lf = jnp.where(row < n_lead_pad, 0.0, lf)
    hi, lo = _split_bf16(lf)
    lower = (lax.broadcasted_iota(I32, (blk, blk), 0) >= lax.broadcasted_iota(I32, (blk, blk), 1)).astype(BF16)
    c = (jnp.dot(lower, hi, preferred_element_type=F32) + jnp.dot(lower, lo, preferred_element_type=F32)
         + carry_ref[...])
    cc = jnp.concatenate(_split_bf16(c * LOG2E, N_SPLIT), axis=1)
    xk_ref[...] = (jnp.dot(cc, pk_ref[...], preferred_element_type=F32) + ok_ref[...]).astype(BF16)
    yq_ref[...] = (jnp.dot(cc, pq_ref[...], preferred_element_type=F32) + oq_ref[...]).astype(BF16)
    carry_ref[...] = c[blk - 1:blk, :]
    last_ref[0] = c[blk - 1:blk, :]


def _fgate_cumsum(f_small, bias_row, init_row, lane_maps, n_batch, n_lead_pad=0, blk=512):
    rows = f_small.shape[0]
    s = rows // n_batch
    blk = min(blk, s)
    nb = s // blk
    pk, pq, ones_k, ones_q = lane_maps
    width = pk.shape[1]
    const = lambda a: pl.BlockSpec(a.shape, lambda b, j: (0, 0))
    return pl.pallas_call(
        functools.partial(_fgate_cumsum_kernel, blk=blk, n_lead_pad=n_lead_pad),
        grid=(n_batch, nb),
        in_specs=[pl.BlockSpec((blk, LANES), lambda b, j: (b * nb + j, 0)),
                  pl.BlockSpec((1, LANES), lambda b, j: (0, 0)),
                  pl.BlockSpec((1, LANES), lambda b, j: (0, 0)),
                  const(pk), const(pq), const(ones_k), const(ones_q)],
        out_specs=[pl.BlockSpec((blk, width), lambda b, j: (b * nb + j, 0)),
                   pl.BlockSpec((blk, width), lambda b, j: (b * nb + j, 0)),
                   pl.BlockSpec((1, 1, LANES), lambda b, j: (b, 0, 0))],
        out_shape=[jax.ShapeDtypeStruct((rows, width), BF16), jax.ShapeDtypeStruct((rows, width), BF16),
                   jax.ShapeDtypeStruct((n_batch, 1, LANES), F32)],
        scratch_shapes=[pltpu.VMEM((1, LANES), F32)],
        compiler_params=_cparams("parallel", "arbitrary"),
        name="fgate_cumsum",
    )(f_small, bias_row, init_row, pk, pq, ones_k, ones_q)


def _attn_kernel(q_ref, k_ref, v_ref, xk_ref, yq_ref, km_ref, vm_ref, xkm_ref, o_ref,
                 kx_ref, vt_ref, m_ref, acc_ref, sa_ref, sb_ref, *, tq, tk, n_q, n_meta_pad):
    lo_half = lax.broadcasted_iota(I32, (tq, LANES), 1) < HEAD_DIM
    krow = lax.broadcasted_iota(I32, (tk, tq), 0)
    qcol = lax.broadcasted_iota(I32, (tk, tq), 1)
    meta_valid = lax.broadcasted_iota(I32, (LANES, tq), 0) >= n_meta_pad
    top_half = lax.broadcasted_iota(I32, (LANES, tq), 0) < HEAD_DIM
    nt = (((1,), (1,)), ((), ()))

    def lo_mask(rows):
        return lax.broadcasted_iota(I32, (rows, LANES), 1) < HEAD_DIM

    def with_ones_t(vals, lo, first):
        ones = jnp.ones(vals.shape, F32)
        vf = vals.astype(F32)
        return (jnp.where(lo, vf, ones) if first else jnp.where(lo, ones, vf)).T.astype(BF16)

    k, v, xk = k_ref[...], v_ref[...], xk_ref[...]
    lo_seq = lo_mask(k.shape[0])
    kx_ref[0] = jnp.where(lo_seq, k, xk)
    kx_ref[1] = jnp.where(lo_seq, xk, k)
    vt_ref[0] = with_ones_t(v, lo_seq, True)
    vt_ref[1] = with_ones_t(v, lo_seq, False)
    km, vm, xkm = km_ref[...], vm_ref[...], xkm_ref[...]
    lo_meta = lo_mask(km.shape[0])
    kmx = (jnp.where(lo_meta, km, xkm), jnp.where(lo_meta, xkm, km))
    vmt = (with_ones_t(vm, lo_meta, True), with_ones_t(vm, lo_meta, False))

    def q_block(qi, carry):
        q0 = pl.multiple_of(qi * tq, tq)
        q = q_ref[pl.ds(q0, tq), :]
        yq = yq_ref[pl.ds(q0, tq), :]
        qx = (jnp.where(lo_half, q, yq), jnp.where(lo_half, yq, q))

        def scores(dst_ref, k0):
            for h in range(2):
                dst_ref[h] = lax.dot_general(kx_ref[h, pl.ds(k0, tk), :], qx[h], nt, preferred_element_type=F32)

        def softmax_pv(src_ref, k0, mask):
            for h in range(2):
                t = src_ref[h] if mask is None else jnp.where(mask, src_ref[h], NEG)
                m_prev = m_ref[h]
                m_new = jnp.maximum(m_prev, jnp.max(t, axis=0, keepdims=True))
                p = jnp.exp2(t - m_new).astype(BF16)
                acc_ref[h] = (jnp.exp2(m_prev - m_new) * acc_ref[h]
                              + jnp.dot(vt_ref[h, :, pl.ds(k0, tk)], p, preferred_element_type=F32))
                m_ref[h] = m_new

        block = lambda b: pl.multiple_of(b * tk, tk)
        causal = krow <= qcol
        scores(sa_ref, block(0))

        for h in range(2):
            t = jnp.where(meta_valid, lax.dot_general(kmx[h], qx[h], nt, preferred_element_type=F32), NEG)
            m0 = jnp.max(t, axis=0, keepdims=True)
            m_ref[h] = m0
            acc_ref[h] = jnp.dot(vmt[h], jnp.exp2(t - m0).astype(BF16), preferred_element_type=F32)

        def block_pair(p, c):
            scores(sb_ref, block(2 * p + 1))
            softmax_pv(sa_ref, block(2 * p), None)
            scores(sa_ref, block(2 * p + 2))
            softmax_pv(sb_ref, block(2 * p + 1), None)
            return c

        lax.fori_loop(0, qi // 2, block_pair, 0)

        @pl.when(qi % 2 == 1)
        def _():
            scores(sb_ref, q0)
            softmax_pv(sa_ref, block(qi - 1), None)
            softmax_pv(sb_ref, q0, causal)

        @pl.when(qi % 2 == 0)
        def _():
            softmax_pv(sa_ref, q0, causal)

        a0, a1 = acc_ref[0], acc_ref[1]
        out_t = jnp.where(top_half, a0 / a0[HEAD_DIM:HEAD_DIM + 1, :], a1 / a1[0:1, :])
        o_ref[pl.ds(q0, tq), :] = out_t.T.astype(o_ref.dtype)
        return carry

    lax.fori_loop(0, n_q, q_block, 0)


def _attention(big, big_meta, xk, yq, xk_meta, n_batch, seq, n_pairs, col_q, col_k, col_v, tq=512, tk=512):
    tq = min(tq, seq)
    tk = min(tk, tq)
    n_q = seq // tq
    mrows = big_meta.shape[0]
    assert tk == tq, "the causal block is a single (tk, tq) block"
    kern = functools.partial(_attn_kernel, tq=tq, tk=tk, n_q=n_q, n_meta_pad=mrows - N_META)
    blk = lambda c0: pl.BlockSpec((seq, LANES), lambda b, p, c0=c0: (b, c0 + p))
    mblk = lambda c0: pl.BlockSpec((mrows, LANES), lambda b, p, c0=c0: (0, c0 + p))
    return pl.pallas_call(
        kern,
        grid=(n_batch, n_pairs),
        in_specs=[blk(col_q), blk(col_k), blk(col_v), blk(0), blk(0), mblk(col_k), mblk(col_v), mblk(0)],
        out_specs=pl.BlockSpec((seq, LANES), lambda b, p: (b, p)),
        out_shape=jax.ShapeDtypeStruct((n_batch * seq, n_pairs * LANES), BF16),
        scratch_shapes=[pltpu.VMEM((2, seq, LANES), BF16), pltpu.VMEM((2, LANES, seq), BF16),
                        pltpu.VMEM((2, 1, tq), F32), pltpu.VMEM((2, LANES, tq), F32),
                        pltpu.VMEM((2, tk, tq), F32), pltpu.VMEM((2, tk, tq), F32)],
        compiler_params=_cparams("parallel", "parallel"),
        name="fox_attention",
    )(big, big, big, xk, yq, big_meta, big_meta, xk_meta)


def _ssd_kernel(x_ref, b_ref, c_ref, z_ref, dt_ref, wx_ref, wb_ref, wc_ref, bx_ref, bb_ref, bc_ref,
                hp_ref, dsk_ref, shift_ref, spread_ref, hx0_ref, hb0_ref, hc0_ref, s0_ref,
                y_ref, sout_ref, hxo_ref, hbo_ref, hco_ref,
                xpad_ref, bpad_ref, cpad_ref, state_ref, *, n_pairs, n_sub, n_lead_pad):
    c = pl.program_id(2)
    n_c = pl.num_programs(2)
    L = CHUNK
    R = n_sub * L
    pg, hpg = n_pairs * LANES, 2 * n_pairs

    @pl.when(c == 0)
    def _():
        xpad_ref[0:HALO, :] = hx0_ref[0]
        bpad_ref[0:HALO, :] = hb0_ref[0]
        cpad_ref[0:HALO, :] = hc0_ref[0]
        state_ref[...] = s0_ref[0, 0]

    n_shift = CONV_WIDTH - 1
    xpad_ref[HALO:HALO + R, :] = x_ref[...]
    bpad_ref[HALO:HALO + R, :] = b_ref[...]
    cpad_ref[HALO:HALO + R, :] = c_ref[...]

    def conv_silu(pad_ref, w_ref, bias_ref, r0):
        taps = jnp.dot(shift_ref[...], pad_ref[r0:r0 + L + HALO, :], preferred_element_type=F32)
        acc = bias_ref[...] + w_ref[n_shift:n_shift + 1, :] * pad_ref[r0 + HALO:r0 + HALO + L, :].astype(F32)
        for kk in range(n_shift):
            acc = acc + w_ref[kk:kk + 1, :] * taps[kk * L:(kk + 1) * L, :]
        return acc * _sigmoid(acc)

    hp = hp_ref[0]
    neg_a = -jnp.exp(hp[1:2, :])
    ri = lax.broadcasted_iota(I32, (L, L), 0)
    ci = lax.broadcasted_iota(I32, (L, L), 1)
    tri = ri >= ci
    tri_bf = tri.astype(BF16)
    lane = lax.broadcasted_iota(I32, (L, LANES), 1)
    lo_half = lane < HEAD_DIM
    lo_row = lo_half[0:1, :]

    state = state_ref[...]
    for i in range(n_sub):
        r0 = i * L
        xs = conv_silu(xpad_ref, wx_ref, bx_ref, r0)
        bm = conv_silu(bpad_ref, wb_ref, bb_ref, r0)
        cm = conv_silu(cpad_ref, wc_ref, bc_ref, r0)

        dt = _softplus(dt_ref[r0:r0 + L, :] + hp[0:1, :])
        if n_lead_pad:
            row = lax.broadcasted_iota(I32, dt.shape, 0) + (c * R + r0)
            dt = jnp.where(row < n_lead_pad, 0.0, dt)
        a = dt * neg_a
        a_hi, a_lo = _split_bf16(a)
        a_cs = (jnp.dot(tri_bf, a_hi, preferred_element_type=F32)
                + jnp.dot(tri_bf, a_lo, preferred_element_type=F32))
        a_cst = a_cs.T
        a_last = a_cs[L - 1:L, :]
        etot = jnp.exp(a_last)
        cols = jnp.where(lane < hpg, dt,
                         jnp.where(lane < 2 * hpg, jnp.exp(a_cs),
                                   jnp.where(lane < 3 * hpg, jnp.exp(a_last - a_cs), a_cs)))
        spread = jnp.dot(jnp.concatenate(_split_bf16(cols), axis=1), spread_ref[...], preferred_element_type=F32)
        dt_x, ea_x, wdec_x = spread[:, 0:pg], spread[:, pg:2 * pg], spread[:, 2 * pg:3 * pg]

        cm_bf = cm.astype(BF16)
        bm_bf = bm.astype(BF16)
        cb = lax.dot_general(cm_bf, bm_bf, (((1,), (1,)), ((), ())), preferred_element_type=F32)
        y_off = jnp.dot(cm_bf, state.astype(BF16), preferred_element_type=F32)

        xw_parts = []
        etot_parts = []
        for p in range(n_pairs):
            ha, hb = 2 * p, 2 * p + 1
            sl = slice(p * LANES, (p + 1) * LANES)
            xs_p = xs[:, sl]
            xdt = xs_p * dt_x[:, sl]
            xdt_bf = xdt.astype(BF16)
            y_heads = []
            for hh in (ha, hb):
                seg = spread[:, 3 * pg + hh * L:3 * pg + (hh + 1) * L] - a_cst[hh:hh + 1, :]
                dec = jnp.where(tri, jnp.exp(seg), 0.0)
                y_heads.append(jnp.dot((cb * dec).astype(BF16), xdt_bf, preferred_element_type=F32))
            y = jnp.where(lo_half, y_heads[0], y_heads[1]) + y_off[:, sl] * ea_x[:, sl] + dsk_ref[:, sl] * xs_p
            zf = z_ref[r0:r0 + L, sl].astype(F32)
            y_ref[r0:r0 + L, sl] = (y * (zf * _sigmoid(zf))).astype(y_ref.dtype)
            xw_parts.append((xdt * wdec_x[:, sl]).astype(BF16))
            etot_parts.append(jnp.where(lo_row, etot[:, ha:ha + 1], etot[:, hb:hb + 1]))

        xw = xw_parts[0] if n_pairs == 1 else jnp.concatenate(xw_parts, axis=1)
        etot_x = etot_parts[0] if n_pairs == 1 else jnp.concatenate(etot_parts, axis=1)
        state = state * etot_x + jnp.dot(bm.T.astype(BF16), xw, preferred_element_type=F32)
    state_ref[...] = state

    xpad_ref[0:HALO, :] = xpad_ref[R:R + HALO, :]
    bpad_ref[0:HALO, :] = bpad_ref[R:R + HALO, :]
    cpad_ref[0:HALO, :] = cpad_ref[R:R + HALO, :]

    @pl.when(c == n_c - 1)
    def _():
        sout_ref[0, 0] = state_ref[...]
        hxo_ref[0] = xpad_ref[0:HALO, :]
        hbo_ref[0] = bpad_ref[0:HALO, :]
        hco_ref[0] = cpad_ref[0:HALO, :]


def _ssd(big, dtg, conv_w, conv_b, head_params, dskip_x, halo_x, halo_b, halo_c, state0,
         n_batch, seq, d_inner, col_z, col_x, n_lead_pad=0):
    G, N, L = SSM_GROUPS, SSM_STATE, CHUNK
    pg = d_inner // G
    n_pairs = pg // LANES
    n_sub = 2 if seq % (2 * L) == 0 else 1
    R = n_sub * L
    n_c = seq // R
    cx, cz = col_x // pg, col_z // pg
    cb0 = (col_x + d_inner) // N
    cc0 = cb0 + G
    row = lambda b, g, c: b * n_c + c
    kern = functools.partial(_ssd_kernel, n_pairs=n_pairs, n_sub=n_sub, n_lead_pad=n_lead_pad)
    n_shift = CONV_WIDTH - 1
    shift_np = np.zeros((n_shift * L, L + HALO), np.float32)
    for kk in range(n_shift):
        shift_np[kk * L + np.arange(L), np.arange(L) + HALO - n_shift + kk] = 1.0
    shift = jnp.asarray(shift_np, BF16)
    hpg = 2 * n_pairs
    spread_np = np.zeros((2, LANES, 3 * pg + hpg * L), np.float32)
    for h in range(hpg):
        for k in range(3):
            spread_np[:, k * hpg + h, k * pg + h * HEAD_DIM:k * pg + (h + 1) * HEAD_DIM] = 1.0
        spread_np[:, 3 * hpg + h, 3 * pg + h * L:3 * pg + (h + 1) * L] = 1.0
    spread = jnp.asarray(spread_np.reshape(2 * LANES, -1), BF16)
    in_specs = [
        pl.BlockSpec((R, pg), lambda b, g, c: (row(b, g, c), cx + g)),
        pl.BlockSpec((R, N), lambda b, g, c: (row(b, g, c), cb0 + g)),
        pl.BlockSpec((R, N), lambda b, g, c: (row(b, g, c), cc0 + g)),
        pl.BlockSpec((R, pg), lambda b, g, c: (row(b, g, c), cz + g)),
        pl.BlockSpec((R, LANES), lambda b, g, c: (row(b, g, c), g)),
        pl.BlockSpec((CONV_WIDTH, pg), lambda b, g, c: (0, g)),
        pl.BlockSpec((CONV_WIDTH, N), lambda b, g, c: (0, d_inner // N + g)),
        pl.BlockSpec((CONV_WIDTH, N), lambda b, g, c: (0, d_inner // N + G + g)),
        pl.BlockSpec((1, pg), lambda b, g, c: (0, g)),
        pl.BlockSpec((1, N), lambda b, g, c: (0, d_inner // N + g)),
        pl.BlockSpec((1, N), lambda b, g, c: (0, d_inner // N + G + g)),
        pl.BlockSpec((1, 8, LANES), lambda b, g, c: (g, 0, 0)),
        pl.BlockSpec((1, pg), lambda b, g, c: (0, g)),
        pl.BlockSpec(shift.shape, lambda b, g, c: (0, 0)),
        pl.BlockSpec(spread.shape, lambda b, g, c: (0, 0)),
        pl.BlockSpec((1, HALO, pg), lambda b, g, c: (0, 0, g)),
        pl.BlockSpec((1, HALO, N), lambda b, g, c: (0, 0, g)),
        pl.BlockSpec((1, HALO, N), lambda b, g, c: (0, 0, g)),
        pl.BlockSpec((1, 1, N, pg), lambda b, g, c: (0, g, 0, 0)),
    ]
    out_specs = [
        pl.BlockSpec((R, pg), lambda b, g, c: (row(b, g, c), g)),
        pl.BlockSpec((1, 1, N, pg), lambda b, g, c: (b, g, 0, 0)),
        pl.BlockSpec((1, HALO, pg), lambda b, g, c: (b, 0, g)),
        pl.BlockSpec((1, HALO, N), lambda b, g, c: (b, 0, g)),
        pl.BlockSpec((1, HALO, N), lambda b, g, c: (b, 0, g)),
    ]
    out_shape = [
        jax.ShapeDtypeStruct((n_batch * seq, d_inner), F32),
        jax.ShapeDtypeStruct((n_batch, G, N, pg), F32),
        jax.ShapeDtypeStruct((n_batch, HALO, d_inner), BF16),
        jax.ShapeDtypeStruct((n_batch, HALO, G * N), BF16),
        jax.ShapeDtypeStruct((n_batch, HALO, G * N), BF16),
    ]
    return pl.pallas_call(
        kern,
        grid=(n_batch, G, n_c),
        in_specs=in_specs,
        out_specs=out_specs,
        out_shape=out_shape,
        scratch_shapes=[pltpu.VMEM((R + HALO, pg), BF16), pltpu.VMEM((R + HALO, N), BF16),
                        pltpu.VMEM((R + HALO, N), BF16), pltpu.VMEM((N, pg), F32)],
        compiler_params=_cparams("parallel", "parallel", "arbitrary"),
        name="ssd_scan",
    )(big, big, big, big, dtg, conv_w, conv_w, conv_w, conv_b, conv_b, conv_b,
      head_params, dskip_x, shift, spread, halo_x, halo_b, halo_c, state0)


def _outproj_kernel(attn_ref, y_ref, x_ref, ga_ref, gs_ref, w_ref, gf_ref, wr_ref, br_ref,
                    h1_ref, n2_ref, logit_ref, *, attn_width):
    def norm(v, g):
        return v * lax.rsqrt(jnp.mean(v * v, axis=-1, keepdims=True) + EPS) * g

    an = norm(attn_ref[...].astype(F32), ga_ref[...]).astype(BF16)
    yn = norm(y_ref[...], gs_ref[...]).astype(BF16)
    mixed = (jnp.dot(an, w_ref[0:attn_width, :], preferred_element_type=F32)
             + jnp.dot(yn, w_ref[attn_width:, :], preferred_element_type=F32))
    h1 = x_ref[...] + mixed
    n2 = norm(h1, gf_ref[...])
    h1_ref[...] = h1
    half = n2.shape[1] // 2
    n2_ref[...] = _pack_bf16_pair(n2[:, :half], n2[:, half:])
    logit_ref[...] = jnp.dot(n2.astype(BF16), wr_ref[...], preferred_element_type=F32) + br_ref[...]


def _outproj(attn, yssm, x, g_attn, g_ssm, w_out, g_ffn, w_router, b_router, tm=512):
    m, d = x.shape
    aw, sw = attn.shape[1], yssm.shape[1]
    tm = min(tm, m)
    row = lambda w: pl.BlockSpec((tm, w), lambda i: (i, 0))
    full = lambda r, c: pl.BlockSpec((r, c), lambda i: (0, 0))
    return pl.pallas_call(
        functools.partial(_outproj_kernel, attn_width=aw),
        grid=(m // tm,),
        in_specs=[row(aw), row(sw), row(d), full(1, aw), full(1, sw), full(aw + sw, d), full(1, d),
                  full(d, LANES), full(1, LANES)],
        out_specs=[row(d), row(d // 2), row(LANES)],
        out_shape=[jax.ShapeDtypeStruct((m, d), F32), jax.ShapeDtypeStruct((m, d // 2), U32),
                   jax.ShapeDtypeStruct((m, LANES), F32)],
        compiler_params=_cparams("parallel"),
        name="outproj_router",
    )(attn, yssm, x, g_attn.reshape(1, aw), g_ssm.reshape(1, sw), w_out, g_ffn.reshape(1, d),
      w_router, b_router)


def _route_kernel(logit_ref, idx_ref, gate_ref, cnt_ref, carry_ref, *, tm):
    i = pl.program_id(0)

    @pl.when(i == 0)
    def _():
        carry_ref[...] = jnp.zeros_like(carry_ref)

    work = logit_ref[...]
    lane = lax.broadcasted_iota(I32, work.shape, 1)
    lane_f = lane.astype(F32)
    vals, ids, hots = [], [], []
    for _ in range(TOP_K):
        mk = jnp.max(work, axis=-1, keepdims=True)
        ik = jnp.min(jnp.where(work == mk, lane_f, float(LANES)), axis=-1, keepdims=True)
        hot = lane_f == ik
        work = jnp.where(hot, -jnp.inf, work)
        vals.append(mk)
        ids.append(ik)
        hots.append(hot)
    exps = [jnp.exp(v - vals[0]) for v in vals]
    denom = exps[0] + exps[1] + exps[2] + exps[3]
    sel = jnp.zeros(work.shape, F32)
    for hot in hots:
        sel = sel + hot.astype(F32)
    strict = (lax.broadcasted_iota(I32, (tm, tm), 1) < lax.broadcasted_iota(I32, (tm, tm), 0)).astype(BF16)
    before = jnp.dot(strict, sel.astype(BF16), preferred_element_type=F32) + carry_ref[...]
    idx_out = jnp.zeros(work.shape, I32)
    gate_out = jnp.zeros(work.shape, F32)
    for k in range(TOP_K):
        rank = jnp.sum(jnp.where(hots[k], before, 0.0), axis=-1, keepdims=True).astype(I32)
        idx_out = jnp.where(lane == k, ids[k].astype(I32), idx_out)
        idx_out = jnp.where(lane == TOP_K + k, rank, idx_out)
        gate_out = jnp.where(lane == k, exps[k] / denom, gate_out)
    idx_ref[...] = idx_out
    gate_ref[...] = gate_out
    carry_ref[...] = carry_ref[...] + jnp.sum(sel, axis=0, keepdims=True)
    cnt_ref[...] = carry_ref[...]


def _route(logits, tm=512):
    m = logits.shape[0]
    tm = min(tm, m)
    return pl.pallas_call(
        functools.partial(_route_kernel, tm=tm),
        grid=(m // tm,),
        in_specs=[pl.BlockSpec((tm, LANES), lambda i: (i, 0))],
        out_specs=[pl.BlockSpec((tm, LANES), lambda i: (i, 0)), pl.BlockSpec((tm, LANES), lambda i: (i, 0)),
                   pl.BlockSpec((1, LANES), lambda i: (0, 0))],
        out_shape=[jax.ShapeDtypeStruct((m, LANES), I32), jax.ShapeDtypeStruct((m, LANES), F32),
                   jax.ShapeDtypeStruct((1, LANES), F32)],
        scratch_shapes=[pltpu.VMEM((1, LANES), F32)],
        compiler_params=_cparams("arbitrary"),
        name="route_top4",
    )(logits)


def _dispatch_kernel(dest_ref, src_ref, init_ref, out_ref, sem, *, td):
    del init_ref

    def issue(j, c):
        for k in range(TOP_K):
            pltpu.make_async_copy(src_ref.at[pl.ds(j, 1)], out_ref.at[pl.ds(dest_ref[j * TOP_K + k], 1)],
                                  sem).start(priority=k % 2)
        return c

    lax.fori_loop(0, td, issue, 0, unroll=8)
    for _ in range(TOP_K):
        pltpu.make_async_copy(src_ref, out_ref.at[pl.ds(0, td)], sem).wait()


def _dispatch(dest_flat, src, n_rows_out, td=512):
    m, d = src.shape
    td = min(td, m)
    init = jnp.zeros((n_rows_out, d), src.dtype)
    return pl.pallas_call(
        functools.partial(_dispatch_kernel, td=td),
        grid=(m // td,),
        in_specs=[pl.BlockSpec((td * TOP_K,), lambda i: (i,), memory_space=pltpu.SMEM),
                  pl.BlockSpec((td, d), lambda i: (i, 0)), pl.BlockSpec(memory_space=pl.ANY)],
        out_specs=pl.BlockSpec(memory_space=pl.ANY),
        out_shape=jax.ShapeDtypeStruct((n_rows_out, d), src.dtype),
        scratch_shapes=[pltpu.SemaphoreType.DMA],
        input_output_aliases={2: 0},
        compiler_params=_cparams("arbitrary"),
        name="moe_dispatch",
    )(dest_flat, src, init)


def _grouped_kernel(te_ref, first_ref, nxt_ref, cidx_ref, cnt_ref, x_hbm, w_hbm, b_ref, o_hbm,
                    xbuf, wstage, wbf, obuf, zbuf, x_sem, w_sem, o_sem, z_sem,
                    *, tm, tn, to, n_sweeps, n_pieces, n_tiles, tile_fn, zero_fn):
    n = pl.program_id(0)
    n_used, n_chg = cnt_ref[0], cnt_ref[1]

    def x_copy(m, slot):
        return pltpu.make_async_copy(x_hbm.at[pl.ds(m * tm, tm)], xbuf.at[slot], x_sem.at[slot])

    def w_copy(e, sweep, slot, p):
        col = pl.multiple_of((p * n_sweeps + sweep) * tn, tn)
        return pltpu.make_async_copy(w_hbm.at[e, :, pl.ds(col, tn)], wstage.at[slot, p], w_sem.at[slot])

    def o_copy(m, slot):
        return pltpu.make_async_copy(obuf.at[slot], o_hbm.at[pl.ds(m * tm, tm), pl.ds(n * to, to)], o_sem.at[slot])

    def z_copy(m):
        return pltpu.make_async_copy(zbuf, o_hbm.at[pl.ds(m * tm, tm), pl.ds(n * to, to)], z_sem)

    @pl.when(n == 0)
    def _():
        zbuf[...] = zero_fn(zbuf.shape)
        x_copy(0, 0).start()
        for p in range(n_pieces):
            w_copy(te_ref[0], 0, 0, p).start()

    def tile(m, carry):
        gm = n * n_used + m
        slot = lax.rem(gm, 2)

        @pl.when(first_ref[m] == 1)
        def _():
            ws = lax.rem(n * n_chg + cidx_ref[m], 2)
            for p in range(n_pieces):
                w_copy(0, 0, ws, p).wait()
            for p in range(n_pieces):
                wbf[p] = wstage[ws, p].astype(BF16)
            nm = nxt_ref[m]

            @pl.when(nm >= 0)
            def _():
                for p in range(n_pieces):
                    w_copy(te_ref[jnp.maximum(nm, 0)], n, 1 - ws, p).start()

            @pl.when(jnp.logical_and(nm < 0, n + 1 < n_sweeps))
            def _():
                for p in range(n_pieces):
                    w_copy(te_ref[0], n + 1, 1 - ws, p).start()

        x_copy(0, slot).wait()

        @pl.when(m + 1 < n_used)
        def _():
            x_copy(m + 1, 1 - slot).start()

        @pl.when(jnp.logical_and(m + 1 >= n_used, n + 1 < n_sweeps))
        def _():
            x_copy(0, 1 - slot).start()

        e = te_ref[m]
        biases = [b_ref[pl.ds(e, 1), pl.ds(pl.multiple_of((p * n_sweeps + n) * tn, tn), tn)]
                  for p in range(n_pieces)]
        res = tile_fn(xbuf[slot], wbf, biases)

        @pl.when(gm >= 2)
        def _():
            o_copy(0, slot).wait()

        obuf[slot] = res
        o_copy(m, slot).start()
        return carry

    lax.fori_loop(0, n_used, tile, 0)

    def zero_tail(m, carry):
        z_copy(m).start()
        return carry

    def zero_tail_wait(m, carry):
        z_copy(0).wait()
        return carry

    lax.fori_loop(n_used, n_tiles, zero_tail, 0)
    lax.fori_loop(n_used, n_tiles, zero_tail_wait, 0)

    @pl.when(n == n_sweeps - 1)
    def _():
        total = n_sweeps * n_used
        for back in (1, 2):
            @pl.when(total >= back)
            def _():
                o_copy(0, lax.rem(total - back, 2)).wait()


def _gate_up_tile(x, wbf, biases):
    lo, hi = _unpack_bf16_pair(x)
    lo, hi = lo.astype(BF16), hi.astype(BF16)
    half = lo.shape[1]

    def proj(p):
        return (jnp.dot(lo, wbf[p, 0:half, :], preferred_element_type=F32)
                + jnp.dot(hi, wbf[p, half:, :], preferred_element_type=F32) + biases[p])

    g = jnp.minimum(proj(0), SWIGLU_LIMIT)
    u = jnp.clip(proj(1), -SWIGLU_LIMIT, SWIGLU_LIMIT)
    return (g * _sigmoid(SWIGLU_ALPHA * g) * (u + 1.0)).astype(BF16)


def _down_tile(h, wbf, biases):
    y = jnp.dot(h, wbf[0], preferred_element_type=F32) + biases[0]
    half = y.shape[1] // 2
    return _pack_bf16_pair(y[:, :half], y[:, half:])


def _packed_zeros(shape):
    zero = jnp.zeros(shape, F32)
    return _pack_bf16_pair(zero, zero)


def _grouped_matmul(plan, x, w, bias, out_dtype, *, tm, tn, to, n_sweeps, n_pieces, tile_fn, zero_fn, name):
    rows, kx = x.shape
    n_tiles = rows // tm
    k = w.shape[1]
    kern = functools.partial(_grouped_kernel, tm=tm, tn=tn, to=to, n_sweeps=n_sweeps, n_pieces=n_pieces,
                             n_tiles=n_tiles, tile_fn=tile_fn, zero_fn=zero_fn)
    any_spec = pl.BlockSpec(memory_space=pl.ANY)
    grid_spec = pltpu.PrefetchScalarGridSpec(
        num_scalar_prefetch=5,
        grid=(n_sweeps,),
        in_specs=[any_spec, any_spec, pl.BlockSpec(bias.shape, lambda n, *_: (0, 0))],
        out_specs=any_spec,
        scratch_shapes=[pltpu.VMEM((2, tm, kx), x.dtype), pltpu.VMEM((2, n_pieces, k, tn), F32),
                        pltpu.VMEM((n_pieces, k, tn), BF16), pltpu.VMEM((2, tm, to), out_dtype),
                        pltpu.VMEM((tm, to), out_dtype), pltpu.SemaphoreType.DMA((2,)),
                        pltpu.SemaphoreType.DMA((2,)), pltpu.SemaphoreType.DMA((2,)), pltpu.SemaphoreType.DMA],
    )
    return pl.pallas_call(
        kern,
        grid_spec=grid_spec,
        out_shape=jax.ShapeDtypeStruct((rows, n_sweeps * to), out_dtype),
        compiler_params=_cparams("arbitrary"),
        name=name,
    )(*plan, x, w, bias)


def _grouped_gate_up(plan, xs, w_gu, b_gu, tm, tn=512):
    ff = w_gu.shape[2] // 2
    tn = min(tn, ff)
    return _grouped_matmul(plan, xs, w_gu, b_gu, BF16, tm=tm, tn=tn, to=tn, n_sweeps=ff // tn, n_pieces=2,
                           tile_fn=_gate_up_tile, zero_fn=lambda s: jnp.zeros(s, BF16), name="moe_gate_up")


def _grouped_down(plan, hdn, w_d, b_d, tm, tn=1024):
    d = w_d.shape[2]
    tn = min(tn, d)
    return _grouped_matmul(plan, hdn, w_d, b_d, U32, tm=tm, tn=tn, to=tn // 2, n_sweeps=d // tn, n_pieces=1,
                           tile_fn=_down_tile, zero_fn=_packed_zeros, name="moe_down")


def _combine_kernel(dest_ref, ys_ref, gate_ref, h1_ref, g_ref, o_ref, buf_ref, sem, *, tc, pack_cols):
    def issue(j, c):
        for k in range(TOP_K):
            pltpu.make_async_copy(ys_ref.at[pl.ds(dest_ref[j * TOP_K + k], 1)], buf_ref.at[k, pl.ds(j, 1)],
                                  sem).start(priority=k % 2)
        return c

    lax.fori_loop(0, tc, issue, 0, unroll=8)
    for k in range(TOP_K):
        pltpu.make_async_copy(ys_ref.at[pl.ds(0, tc)], buf_ref.at[k], sem).wait()

    gates = gate_ref[...]
    hw = pack_cols // 2
    cols = []
    for n in range(buf_ref.shape[2] // hw):
        lo_acc = h1_ref[:, n * pack_cols:n * pack_cols + hw]
        hi_acc = h1_ref[:, n * pack_cols + hw:(n + 1) * pack_cols]
        for k in range(TOP_K):
            lo, hi = _unpack_bf16_pair(buf_ref[k, :, n * hw:(n + 1) * hw])
            lo_acc = lo_acc + gates[:, k:k + 1] * lo
            hi_acc = hi_acc + gates[:, k:k + 1] * hi
        cols += [lo_acc, hi_acc]
    h2 = jnp.concatenate(cols, axis=1)
    o_ref[...] = h2 * lax.rsqrt(jnp.mean(h2 * h2, axis=-1, keepdims=True) + EPS) * g_ref[...]


def _combine(dest_flat, ys, gates, h1, g_final, pack_cols, tc=512):
    m, d = h1.shape
    tc = min(tc, m)
    return pl.pallas_call(
        functools.partial(_combine_kernel, tc=tc, pack_cols=pack_cols),
        grid=(m // tc,),
        in_specs=[pl.BlockSpec((tc * TOP_K,), lambda i: (i,), memory_space=pltpu.SMEM),
                  pl.BlockSpec(memory_space=pl.ANY),
                  pl.BlockSpec((tc, LANES), lambda i: (i, 0)),
                  pl.BlockSpec((tc, d), lambda i: (i, 0)),
                  pl.BlockSpec((1, d), lambda i: (0, 0))],
        out_specs=pl.BlockSpec((tc, d), lambda i: (i, 0)),
        out_shape=jax.ShapeDtypeStruct((m, d), F32),
        scratch_shapes=[pltpu.VMEM((TOP_K, tc, d // 2), U32), pltpu.SemaphoreType.DMA],
        compiler_params=_cparams("arbitrary"),
        name="moe_combine",
    )(dest_flat, ys, gates, h1, g_final.reshape(1, d))


def _lane_pad(v, width=LANES):
    return jnp.pad(v, ((0, 0), (0, width - v.shape[1])))


def kernel(x, meta_tokens, norm_mix_g, w_in, fgate_b, attn_norm_g, conv_w, conv_b, dt_bias, a_log, d_skip,
           ssm_norm_g, w_out, norm_ffn_g, w_router, b_router, w_gate_up, b_gate_up, w_down, b_down,
           final_norm_g):
    n_batch, seq, d = x.shape
    aw = (d // 128) * HEAD_DIM
    n_ah = aw // HEAD_DIM
    sw = d
    n_sh = sw // HEAD_DIM
    hpg = n_sh // SSM_GROUPS
    gn = SSM_GROUPS * SSM_STATE
    tokens = n_batch * seq
    lyr = 0

    o_f = 3 * aw
    o_z = o_f + n_ah
    o_xbc = o_z + sw
    o_dt = o_xbc + sw + 2 * gn
    wt = jnp.swapaxes(w_in[lyr], 0, 1)
    w_q = wt[:aw] * (LOG2E * HEAD_DIM ** -0.5)
    w_big = jnp.concatenate([w_q, wt[aw:o_f], wt[o_z:o_dt]], axis=0).astype(BF16)
    w_dt = jnp.tile(wt[o_dt:o_dt + n_sh].reshape(SSM_GROUPS, hpg, d), (1, HEAD_COPIES, 1))
    w_dt = jnp.pad(w_dt, ((0, 0), (0, LANES - HEAD_COPIES * hpg), (0, 0))).reshape(SSM_GROUPS * LANES, d)
    w_f = jnp.pad(wt[o_f:o_z], ((0, LANES - n_ah), (0, 0)))
    w_small = jnp.concatenate([w_f, w_dt], axis=0).astype(BF16)
    col_k, col_v, col_z, col_x = aw, 2 * aw, 3 * aw, 3 * aw + sw

    meta_rows = jnp.concatenate([jnp.zeros((CHUNK - N_META, d), F32), meta_tokens.astype(F32)], axis=0)
    g_mix = norm_mix_g[lyr]
    n_meta = _rmsnorm_rows(meta_rows, g_mix, CHUNK)
    n_real = _rmsnorm_rows(x.reshape(tokens, d), g_mix, 512)
    big_meta = _matmul(n_meta, w_big, BF16, CHUNK, 1024, "inproj_meta")
    small_meta = _matmul(n_meta, w_small, F32, CHUNK, 640, "inproj_small_meta")
    big = _matmul(n_real, w_big, BF16, 1024, 1024, "inproj")
    small = _matmul(n_real, w_small, F32, 1024, 640, "inproj_small")

    fb = _lane_pad(fgate_b[lyr].reshape(1, n_ah).astype(F32))
    lane_maps = _gate_lane_maps(n_ah)
    xk_meta, _, c_meta_last = _fgate_cumsum(small_meta[:, :LANES], fb, jnp.zeros((1, LANES), F32), lane_maps, 1,
                                            n_lead_pad=CHUNK - N_META)
    xk, yq, _ = _fgate_cumsum(small[:, :LANES], fb, c_meta_last[0], lane_maps, n_batch)
    attn = _attention(big, big_meta, xk, yq, xk_meta, n_batch, seq, n_ah // 2, 0, col_k // LANES,
                      col_v // LANES)

    head_params = jnp.zeros((SSM_GROUPS, 8, LANES), F32)
    per_group = lambda v: jnp.tile(v.reshape(SSM_GROUPS, hpg).astype(F32), (1, HEAD_COPIES))
    head_params = head_params.at[:, 0, :HEAD_COPIES * hpg].set(per_group(dt_bias[lyr]))
    head_params = head_params.at[:, 1, :HEAD_COPIES * hpg].set(per_group(a_log[lyr]))
    dskip_x = jnp.repeat(d_skip[lyr].astype(F32), HEAD_DIM).reshape(1, sw)
    cw, cb = conv_w[lyr].astype(F32), conv_b[lyr].astype(F32).reshape(1, -1)
    dtg_meta, dtg = small_meta[:, LANES:], small[:, LANES:]
    pg = sw // SSM_GROUPS
    zeros_h = lambda w: jnp.zeros((1, HALO, w), BF16)
    state0 = jnp.zeros((1, SSM_GROUPS, SSM_STATE, pg), F32)
    _, s_meta, hx, hb, hc = _ssd(big_meta, dtg_meta, cw, cb, head_params, dskip_x, zeros_h(sw), zeros_h(gn),
                                 zeros_h(gn), state0, 1, CHUNK, sw, col_z, col_x, n_lead_pad=CHUNK - N_META)
    yssm = _ssd(big, dtg, cw, cb, head_params, dskip_x, hx, hb, hc, s_meta, n_batch, seq, sw, col_z, col_x)[0]

    wr = _lane_pad(w_router[lyr]).astype(BF16)
    br = jnp.concatenate([b_router[lyr].astype(F32), jnp.full((LANES - N_EXPERTS,), NEG, F32)]).reshape(1, LANES)
    h1, n2, logits = _outproj(attn, yssm, x.reshape(tokens, d), attn_norm_g[lyr], ssm_norm_g[lyr],
                              w_out[lyr].astype(BF16), norm_ffn_g[lyr], wr, br)

    idx, gates, counts = _route(logits)
    tm = min(512, tokens)
    eid, rank = idx[:, :TOP_K], idx[:, TOP_K:2 * TOP_K]
    cnt = counts[0, :N_EXPERTS].astype(I32)
    padded = ((cnt + tm - 1) // tm) * tm
    ends = jnp.cumsum(padded)
    starts = ends - padded
    dest = (starts[eid] + rank).reshape(-1)
    n_tiles = (tokens * TOP_K + N_EXPERTS * (tm - 1)) // tm
    tile_row0 = jnp.arange(n_tiles, dtype=I32) * tm
    tile_expert = jnp.minimum(jnp.sum((ends[None, :] <= tile_row0[:, None]).astype(I32), axis=1), N_EXPERTS - 1)
    n_used = (ends[-1] // tm).astype(I32)
    tile_id = jnp.arange(n_tiles, dtype=I32)
    first = (jnp.concatenate([jnp.ones((1,), bool), tile_expert[1:] != tile_expert[:-1]])
             & (tile_id < n_used)).astype(I32)
    cidx = jnp.cumsum(first) - 1
    later_first = (tile_id[None, :] > tile_id[:, None]) & (first[None, :] == 1)
    nxt = jnp.min(jnp.where(later_first, tile_id[None, :], n_tiles), axis=1)
    nxt = jnp.where(nxt >= n_tiles, -1, nxt).astype(I32)
    plan = (tile_expert, first, nxt, cidx.astype(I32), jnp.stack([n_used, jnp.sum(first)]).astype(I32))

    xs_sorted = _dispatch(dest, n2, n_tiles * tm)
    hdn = _grouped_gate_up(plan, xs_sorted, w_gate_up[lyr], b_gate_up[lyr].astype(F32), tm)
    tn_down = min(1024, d)
    ys = _grouped_down(plan, hdn, w_down[lyr], b_down[lyr].astype(F32), tm, tn=tn_down)
    out = _combine(dest, ys, gates, h1, final_norm_g, tn_down)
    return out.reshape(n_batch, seq, d)
```

```python
import functools

import numpy as np
import jax
import jax.numpy as jnp
from jax import lax
from jax.experimental import pallas as pl
from jax.experimental.pallas import tpu as pltpu

F32 = jnp.float32
BF16 = jnp.bfloat16
I32 = jnp.int32
U32 = jnp.uint32

N_META = 16
CHUNK = 128
HEAD_DIM = 64
SSM_GROUPS = 4
SSM_STATE = 128
CONV_WIDTH = 4
N_EXPERTS = 32
TOP_K = 4
SWIGLU_LIMIT = 7.0
SWIGLU_ALPHA = 1.702
EPS = 1e-5
LANES = 128
HALO = 16
NEG = -1e30
LOG2E = 1.4426950408889634
N_SPLIT = 3
HEAD_COPIES = 4
VMEM_LIMIT_BYTES = 56 * 1024 * 1024


def _cparams(*sem):
    return pltpu.CompilerParams(dimension_semantics=sem, vmem_limit_bytes=VMEM_LIMIT_BYTES)


def _sigmoid(x):
    return 0.5 * jnp.tanh(0.5 * x) + 0.5


def _softplus(x):
    return jnp.maximum(x, 0.0) + jnp.log(1.0 + jnp.exp(-jnp.abs(x)))


def _pack_bf16_pair(lo, hi):
    return pltpu.pack_elementwise([lo, hi], packed_dtype=BF16)


def _unpack_bf16_pair(w):
    lo = pltpu.unpack_elementwise(w, index=0, packed_dtype=BF16, unpacked_dtype=F32)
    hi = pltpu.unpack_elementwise(w, index=1, packed_dtype=BF16, unpacked_dtype=F32)
    return lo, hi


def _split_bf16(x, n=2):
    parts = []
    for _ in range(n):
        p = x.astype(BF16)
        parts.append(p)
        x = x - p.astype(F32)
    return parts


def _rmsnorm_kernel(x_ref, g_ref, o_ref):
    x = x_ref[...]
    ms = jnp.mean(x * x, axis=-1, keepdims=True)
    o_ref[...] = (x * lax.rsqrt(ms + EPS) * g_ref[...]).astype(o_ref.dtype)


def _rmsnorm_rows(x, g, tm):
    m, d = x.shape
    tm = min(tm, m)
    return pl.pallas_call(
        _rmsnorm_kernel,
        grid=(m // tm,),
        in_specs=[pl.BlockSpec((tm, d), lambda i: (i, 0)), pl.BlockSpec((1, d), lambda i: (0, 0))],
        out_specs=pl.BlockSpec((tm, d), lambda i: (i, 0)),
        out_shape=jax.ShapeDtypeStruct((m, d), BF16),
        compiler_params=_cparams("parallel"),
        name="rmsnorm_rows",
    )(x, g.reshape(1, d))


def _matmul_kernel(a_ref, wt_ref, o_ref):
    o_ref[...] = lax.dot_general(a_ref[...], wt_ref[...], (((1,), (1,)), ((), ())),
                                 preferred_element_type=F32).astype(o_ref.dtype)


def _pick_tile(n, pref, unit=LANES):
    best = unit
    for t in range(unit, min(pref, n) + 1, unit):
        if n % t == 0:
            best = t
    return best


def _matmul(a, wt, out_dtype, tm, tn, name):
    m, k = a.shape
    n = wt.shape[0]
    tm, tn = min(tm, m), _pick_tile(n, tn)
    return pl.pallas_call(
        _matmul_kernel,
        grid=(n // tn, m // tm),
        in_specs=[pl.BlockSpec((tm, k), lambda j, i: (i, 0)), pl.BlockSpec((tn, k), lambda j, i: (j, 0))],
        out_specs=pl.BlockSpec((tm, tn), lambda j, i: (i, j)),
        out_shape=jax.ShapeDtypeStruct((m, n), out_dtype),
        compiler_params=_cparams("parallel", "parallel"),
        name=name,
    )(a, wt)


def _gate_lane_maps(n_heads):
    n_pairs = n_heads // 2
    pk = np.zeros((N_SPLIT * LANES, n_pairs * LANES), np.float32)
    pq = np.zeros_like(pk)
    ones_k = np.zeros((1, n_pairs * LANES), np.float32)
    ones_q = np.zeros_like(ones_k)
    for h in range(n_heads):
        base = (h // 2) * LANES + (HEAD_DIM if h % 2 == 0 else 0)
        for i in range(N_SPLIT):
            pk[i * LANES + h, base + i] = -1.0
            pq[i * LANES + h, base + N_SPLIT + i] = 1.0
            ones_q[0, base + i] = 1.0
            ones_k[0, base + N_SPLIT + i] = 1.0
    return (jnp.asarray(pk, BF16), jnp.asarray(pq, BF16), jnp.asarray(ones_k), jnp.asarray(ones_q))


def _fgate_cumsum_kernel(f_ref, b_ref, init_ref, pk_ref, pq_ref, ok_ref, oq_ref, xk_ref, yq_ref, last_ref,
                         carry_ref, *, blk, n_lead_pad):
    j = pl.program_id(1)

    @pl.when(j == 0)
    def _():
        carry_ref[...] = init_ref[...]

    x = f_ref[...] + b_ref[...]
    lf = jnp.minimum(x, 0.0) - jnp.log(1.0 + jnp.exp(-jnp.abs(x)))
    if n_lead_pad:
        row = lax.broadcasted_iota(I32, lf.shape, 0) + j * blk
        lf = jnp.where(row < n_lead_pad, 0.0, lf)
    hi, lo = _split_bf16(lf)
    lower = (lax.broadcasted_iota(I32, (blk, blk), 0) >= lax.broadcasted_iota(I32, (blk, blk), 1)).astype(BF16)
    c = (jnp.dot(lower, hi, preferred_element_type=F32) + jnp.dot(lower, lo, preferred_element_type=F32)
         + carry_ref[...])
    cc = jnp.concatenate(_split_bf16(c * LOG2E, N_SPLIT), axis=1)
    xk_ref[...] = (jnp.dot(cc, pk_ref[...], preferred_element_type=F32) + ok_ref[...]).astype(BF16)
    yq_ref[...] = (jnp.dot(cc, pq_ref[...], preferred_element_type=F32) + oq_ref[...]).astype(BF16)
    carry_ref[...] = c[blk - 1:blk, :]
    last_ref[0] = c[blk - 1:blk, :]


def _fgate_cumsum(f_small, bias_row, init_row, lane_maps, n_batch, n_lead_pad=0, blk=512):
    rows = f_small.shape[0]
    s = rows // n_batch
    blk = min(blk, s)
    nb = s // blk
    pk, pq, ones_k, ones_q = lane_maps
    width = pk.shape[1]
    const = lambda a: pl.BlockSpec(a.shape, lambda b, j: (0, 0))
    return pl.pallas_call(
        functools.partial(_fgate_cumsum_kernel, blk=blk, n_lead_pad=n_lead_pad),
        grid=(n_batch, nb),
        in_specs=[pl.BlockSpec((blk, LANES), lambda b, j: (b * nb + j, 0)),
                  pl.BlockSpec((1, LANES), lambda b, j: (0, 0)),
                  pl.BlockSpec((1, LANES), lambda b, j: (0, 0)),
                  const(pk), const(pq), const(ones_k), const(ones_q)],
        out_specs=[pl.BlockSpec((blk, width), lambda b, j: (b * nb + j, 0)),
                   pl.BlockSpec((blk, width), lambda b, j: (b * nb + j, 0)),
                   pl.BlockSpec((1, 1, LANES), lambda b, j: (b, 0, 0))],
        out_shape=[jax.ShapeDtypeStruct((rows, width), BF16), jax.ShapeDtypeStruct((rows, width), BF16),
                   jax.ShapeDtypeStruct((n_batch, 1, LANES), F32)],
        scratch_shapes=[pltpu.VMEM((1, LANES), F32)],
        compiler_params=_cparams("parallel", "arbitrary"),
        name="fgate_cumsum",
    )(f_small, bias_row, init_row, pk, pq, ones_k, ones_q)


def _attn_kernel(q_ref, k_ref, v_ref, xk_ref, yq_ref, km_ref, vm_ref, xkm_ref, o_ref,
                 kx_ref, vt_ref, m_ref, acc_ref, sa_ref, sb_ref, *, tq, tk, n_q, n_meta_pad):
    lo_half = lax.broadcasted_iota(I32, (tq, LANES), 1) < HEAD_DIM
    krow = lax.broadcasted_iota(I32, (tk, tq), 0)
    qcol = lax.broadcasted_iota(I32, (tk, tq), 1)
    meta_valid = lax.broadcasted_iota(I32, (LANES, tq), 0) >= n_meta_pad
    top_half = lax.broadcasted_iota(I32, (LANES, tq), 0) < HEAD_DIM
    nt = (((1,), (1,)), ((), ()))

    def lo_mask(rows):
        return lax.broadcasted_iota(I32, (rows, LANES), 1) < HEAD_DIM

    def with_ones_t(vals, lo, first):
        ones = jnp.ones(vals.shape, F32)
        vf = vals.astype(F32)
        return (jnp.where(lo, vf, ones) if first else jnp.where(lo, ones, vf)).T.astype(BF16)

    k, v, xk = k_ref[...], v_ref[...], xk_ref[...]
    lo_seq = lo_mask(k.shape[0])
    kx_ref[0] = jnp.where(lo_seq, k, xk)
    kx_ref[1] = jnp.where(lo_seq, xk, k)
    vt_ref[0] = with_ones_t(v, lo_seq, True)
    vt_ref[1] = with_ones_t(v, lo_seq, False)
    km, vm, xkm = km_ref[...], vm_ref[...], xkm_ref[...]
    lo_meta = lo_mask(km.shape[0])
    kmx = (jnp.where(lo_meta, km, xkm), jnp.where(lo_meta, xkm, km))
    vmt = (with_ones_t(vm, lo_meta, True), with_ones_t(vm, lo_meta, False))

    def q_block(qi, carry):
        q0 = pl.multiple_of(qi * tq, tq)
        q = q_ref[pl.ds(q0, tq), :]
        yq = yq_ref[pl.ds(q0, tq), :]
        qx = (jnp.where(lo_half, q, yq), jnp.where(lo_half, yq, q))

        def scores(dst_ref, k0):
            for h in range(2):
                dst_ref[h] = lax.dot_general(kx_ref[h, pl.ds(k0, tk), :], qx[h], nt, preferred_element_type=F32)

        def softmax_pv(src_ref, k0, mask):
            for h in range(2):
                t = src_ref[h] if mask is None else jnp.where(mask, src_ref[h], NEG)
                m_prev = m_ref[h]
                m_new = jnp.maximum(m_prev, jnp.max(t, axis=0, keepdims=True))
                p = jnp.exp2(t - m_new).astype(BF16)
                acc_ref[h] = (jnp.exp2(m_prev - m_new) * acc_ref[h]
                              + jnp.dot(vt_ref[h, :, pl.ds(k0, tk)], p, preferred_element_type=F32))
                m_ref[h] = m_new

        block = lambda b: pl.multiple_of(b * tk, tk)
        causal = krow <= qcol
        scores(sa_ref, block(0))

        for h in range(2):
            t = jnp.where(meta_valid, lax.dot_general(kmx[h], qx[h], nt, preferred_element_type=F32), NEG)
            m0 = jnp.max(t, axis=0, keepdims=True)
            m_ref[h] = m0
            acc_ref[h] = jnp.dot(vmt[h], jnp.exp2(t - m0).astype(BF16), preferred_element_type=F32)

        def block_pair(p, c):
            scores(sb_ref, block(2 * p + 1))
            softmax_pv(sa_ref, block(2 * p), None)
            scores(sa_ref, block(2 * p + 2))
            softmax_pv(sb_ref, block(2 * p + 1), None)
            return c

        lax.fori_loop(0, qi // 2, block_pair, 0)

        @pl.when(qi % 2 == 1)
        def _():
            scores(sb_ref, q0)
            softmax_pv(sa_ref, block(qi - 1), None)
            softmax_pv(sb_ref, q0, causal)

        @pl.when(qi % 2 == 0)
        def _():
            softmax_pv(sa_ref, q0, causal)

        a0, a1 = acc_ref[0], acc_ref[1]
        out_t = jnp.where(top_half, a0 / a0[HEAD_DIM:HEAD_DIM + 1, :], a1 / a1[0:1, :])
        o_ref[pl.ds(q0, tq), :] = out_t.T.astype(o_ref.dtype)
        return carry

    lax.fori_loop(0, n_q, q_block, 0)


def _attention(big, big_meta, xk, yq, xk_meta, n_batch, seq, n_pairs, col_q, col_k, col_v, tq=512, tk=512):
    tq = min(tq, seq)
    tk = min(tk, tq)
    n_q = seq // tq
    mrows = big_meta.shape[0]
    assert tk == tq, "the causal block is a single (tk, tq) block"
    kern = functools.partial(_attn_kernel, tq=tq, tk=tk, n_q=n_q, n_meta_pad=mrows - N_META)
    blk = lambda c0: pl.BlockSpec((seq, LANES), lambda b, p, c0=c0: (b, c0 + p))
    mblk = lambda c0: pl.BlockSpec((mrows, LANES), lambda b, p, c0=c0: (0, c0 + p))
    return pl.pallas_call(
        kern,
        grid=(n_batch, n_pairs),
        in_specs=[blk(col_q), blk(col_k), blk(col_v), blk(0), blk(0), mblk(col_k), mblk(col_v), mblk(0)],
        out_specs=pl.BlockSpec((seq, LANES), lambda b, p: (b, p)),
        out_shape=jax.ShapeDtypeStruct((n_batch * seq, n_pairs * LANES), BF16),
        scratch_shapes=[pltpu.VMEM((2, seq, LANES), BF16), pltpu.VMEM((2, LANES, seq), BF16),
                        pltpu.VMEM((2, 1, tq), F32), pltpu.VMEM((2, LANES, tq), F32),
                        pltpu.VMEM((2, tk, tq), F32), pltpu.VMEM((2, tk, tq), F32)],
        compiler_params=_cparams("parallel", "parallel"),
        name="fox_attention",
    )(big, big, big, xk, yq, big_meta, big_meta, xk_meta)


def _ssd_kernel(x_ref, b_ref, c_ref, z_ref, dt_ref, wx_ref, wb_ref, wc_ref, bx_ref, bb_ref, bc_ref,
                hp_ref, dsk_ref, shift_ref, spread_ref, hx0_ref, hb0_ref, hc0_ref, s0_ref,
                y_ref, sout_ref, hxo_ref, hbo_ref, hco_ref,
                xpad_ref, bpad_ref, cpad_ref, state_ref, *, n_pairs, n_sub, n_lead_pad):
    c = pl.program_id(2)
    n_c = pl.num_programs(2)
    L = CHUNK
    R = n_sub * L
    pg, hpg = n_pairs * LANES, 2 * n_pairs

    @pl.when(c == 0)
    def _():
        xpad_ref[0:HALO, :] = hx0_ref[0]
        bpad_ref[0:HALO, :] = hb0_ref[0]
        cpad_ref[0:HALO, :] = hc0_ref[0]
        state_ref[...] = s0_ref[0, 0]

    n_shift = CONV_WIDTH - 1
    xpad_ref[HALO:HALO + R, :] = x_ref[...]
    bpad_ref[HALO:HALO + R, :] = b_ref[...]
    cpad_ref[HALO:HALO + R, :] = c_ref[...]

    def conv_silu(pad_ref, w_ref, bias_ref, r0):
        taps = jnp.dot(shift_ref[...], pad_ref[r0:r0 + L + HALO, :], preferred_element_type=F32)
        acc = bias_ref[...] + w_ref[n_shift:n_shift + 1, :] * pad_ref[r0 + HALO:r0 + HALO + L, :].astype(F32)
        for kk in range(n_shift):
            acc = acc + w_ref[kk:kk + 1, :] * taps[kk * L:(kk + 1) * L, :]
        return acc * _sigmoid(acc)

    hp = hp_ref[0]
    neg_a = -jnp.exp(hp[1:2, :])
    ri = lax.broadcasted_iota(I32, (L, L), 0)
    ci = lax.broadcasted_iota(I32, (L, L), 1)
    tri = ri >= ci
    tri_bf = tri.astype(BF16)
    lane = lax.broadcasted_iota(I32, (L, LANES), 1)
    lo_half = lane < HEAD_DIM
    lo_row = lo_half[0:1, :]

    state = state_ref[...]
    for i in range(n_sub):
        r0 = i * L
        xs = conv_silu(xpad_ref, wx_ref, bx_ref, r0)
        bm = conv_silu(bpad_ref, wb_ref, bb_ref, r0)
        cm = conv_silu(cpad_ref, wc_ref, bc_ref, r0)

        dt = _softplus(dt_ref[r0:r0 + L, :] + hp[0:1, :])
        if n_lead_pad:
            row = lax.broadcasted_iota(I32, dt.shape, 0) + (c * R + r0)
            dt = jnp.where(row < n_lead_pad, 0.0, dt)
        a = dt * neg_a
        a_hi, a_lo = _split_bf16(a)
        a_cs = (jnp.dot(tri_bf, a_hi, preferred_element_type=F32)
                + jnp.dot(tri_bf, a_lo, preferred_element_type=F32))
        a_cst = a_cs.T
        a_last = a_cs[L - 1:L, :]
        etot = jnp.exp(a_last)
        cols = jnp.where(lane < hpg, dt,
                         jnp.where(lane < 2 * hpg, jnp.exp(a_cs),
                                   jnp.where(lane < 3 * hpg, jnp.exp(a_last - a_cs), a_cs)))
        spread = jnp.dot(jnp.concatenate(_split_bf16(cols), axis=1), spread_ref[...], preferred_element_type=F32)
        dt_x, ea_x, wdec_x = spread[:, 0:pg], spread[:, pg:2 * pg], spread[:, 2 * pg:3 * pg]

        cm_bf = cm.astype(BF16)
        bm_bf = bm.astype(BF16)
        cb = lax.dot_general(cm_bf, bm_bf, (((1,), (1,)), ((), ())), preferred_element_type=F32)
        y_off = jnp.dot(cm_bf, state.astype(BF16), preferred_element_type=F32)

        xw_parts = []
        etot_parts = []
        for p in range(n_pairs):
            ha, hb = 2 * p, 2 * p + 1
            sl = slice(p * LANES, (p + 1) * LANES)
            xs_p = xs[:, sl]
            xdt = xs_p * dt_x[:, sl]
            xdt_bf = xdt.astype(BF16)
            y_heads = []
            for hh in (ha, hb):
                seg = spread[:, 3 * pg + hh * L:3 * pg + (hh + 1) * L] - a_cst[hh:hh + 1, :]
                dec = jnp.where(tri, jnp.exp(seg), 0.0)
                y_heads.append(jnp.dot((cb * dec).astype(BF16), xdt_bf, preferred_element_type=F32))
            y = jnp.where(lo_half, y_heads[0], y_heads[1]) + y_off[:, sl] * ea_x[:, sl] + dsk_ref[:, sl] * xs_p
            zf = z_ref[r0:r0 + L, sl].astype(F32)
            y_ref[r0:r0 + L, sl] = (y * (zf * _sigmoid(zf))).astype(y_ref.dtype)
            xw_parts.append((xdt * wdec_x[:, sl]).astype(BF16))
            etot_parts.append(jnp.where(lo_row, etot[:, ha:ha + 1], etot[:, hb:hb + 1]))

        xw = xw_parts[0] if n_pairs == 1 else jnp.concatenate(xw_parts, axis=1)
        etot_x = etot_parts[0] if n_pairs == 1 else jnp.concatenate(etot_parts, axis=1)
        state = state * etot_x + jnp.dot(bm.T.astype(BF16), xw, preferred_element_type=F32)
    state_ref[...] = state

    xpad_ref[0:HALO, :] = xpad_ref[R:R + HALO, :]
    bpad_ref[0:HALO, :] = bpad_ref[R:R + HALO, :]
    cpad_ref[0:HALO, :] = cpad_ref[R:R + HALO, :]

    @pl.when(c == n_c - 1)
    def _():
        sout_ref[0, 0] = state_ref[...]
        hxo_ref[0] = xpad_ref[0:HALO, :]
        hbo_ref[0] = bpad_ref[0:HALO, :]
        hco_ref[0] = cpad_ref[0:HALO, :]


def _ssd(big, dtg, conv_w, conv_b, head_params, dskip_x, halo_x, halo_b, halo_c, state0,
         n_batch, seq, d_inner, col_z, col_x, n_lead_pad=0):
    G, N, L = SSM_GROUPS, SSM_STATE, CHUNK
    pg = d_inner // G
    n_pairs = pg // LANES
    n_sub = 2 if seq % (2 * L) == 0 else 1
    R = n_sub * L
    n_c = seq // R
    cx, cz = col_x // pg, col_z // pg
    cb0 = (col_x + d_inner) // N
    cc0 = cb0 + G
    row = lambda b, g, c: b * n_c + c
    kern = functools.partial(_ssd_kernel, n_pairs=n_pairs, n_sub=n_sub, n_lead_pad=n_lead_pad)
    n_shift = CONV_WIDTH - 1
    shift_np = np.zeros((n_shift * L, L + HALO), np.float32)
    for kk in range(n_shift):
        shift_np[kk * L + np.arange(L), np.arange(L) + HALO - n_shift + kk] = 1.0
    shift = jnp.asarray(shift_np, BF16)
    hpg = 2 * n_pairs
    spread_np = np.zeros((2, LANES, 3 * pg + hpg * L), np.float32)
    for h in range(hpg):
        for k in range(3):
            spread_np[:, k * hpg + h, k * pg + h * HEAD_DIM:k * pg + (h + 1) * HEAD_DIM] = 1.0
        spread_np[:, 3 * hpg + h, 3 * pg + h * L:3 * pg + (h + 1) * L] = 1.0
    spread = jnp.asarray(spread_np.reshape(2 * LANES, -1), BF16)
    in_specs = [
        pl.BlockSpec((R, pg), lambda b, g, c: (row(b, g, c), cx + g)),
        pl.BlockSpec((R, N), lambda b, g, c: (row(b, g, c), cb0 + g)),
        pl.BlockSpec((R, N), lambda b, g, c: (row(b, g, c), cc0 + g)),
        pl.BlockSpec((R, pg), lambda b, g, c: (row(b, g, c), cz + g)),
        pl.BlockSpec((R, LANES), lambda b, g, c: (row(b, g, c), g)),
        pl.BlockSpec((CONV_WIDTH, pg), lambda b, g, c: (0, g)),
        pl.BlockSpec((CONV_WIDTH, N), lambda b, g, c: (0, d_inner // N + g)),
        pl.BlockSpec((CONV_WIDTH, N), lambda b, g, c: (0, d_inner // N + G + g)),
        pl.BlockSpec((1, pg), lambda b, g, c: (0, g)),
        pl.BlockSpec((1, N), lambda b, g, c: (0, d_inner // N + g)),
        pl.BlockSpec((1, N), lambda b, g, c: (0, d_inner // N + G + g)),
        pl.BlockSpec((1, 8, LANES), lambda b, g, c: (g, 0, 0)),
        pl.BlockSpec((1, pg), lambda b, g, c: (0, g)),
        pl.BlockSpec(shift.shape, lambda b, g, c: (0, 0)),
        pl.BlockSpec(spread.shape, lambda b, g, c: (0, 0)),
        pl.BlockSpec((1, HALO, pg), lambda b, g, c: (0, 0, g)),
        pl.BlockSpec((1, HALO, N), lambda b, g, c: (0, 0, g)),
        pl.BlockSpec((1, HALO, N), lambda b, g, c: (0, 0, g)),
        pl.BlockSpec((1, 1, N, pg), lambda b, g, c: (0, g, 0, 0)),
    ]
    out_specs = [
        pl.BlockSpec((R, pg), lambda b, g, c: (row(b, g, c), g)),
        pl.BlockSpec((1, 1, N, pg), lambda b, g, c: (b, g, 0, 0)),
        pl.BlockSpec((1, HALO, pg), lambda b, g, c: (b, 0, g)),
        pl.BlockSpec((1, HALO, N), lambda b, g, c: (b, 0, g)),
        pl.BlockSpec((1, HALO, N), lambda b, g, c: (b, 0, g)),
    ]
    out_shape = [
        jax.ShapeDtypeStruct((n_batch * seq, d_inner), F32),
        jax.ShapeDtypeStruct((n_batch, G, N, pg), F32),
        jax.ShapeDtypeStruct((n_batch, HALO, d_inner), BF16),
        jax.ShapeDtypeStruct((n_batch, HALO, G * N), BF16),
        jax.ShapeDtypeStruct((n_batch, HALO, G * N), BF16),
    ]
    return pl.pallas_call(
        kern,
        grid=(n_batch, G, n_c),
        in_specs=in_specs,
        out_specs=out_specs,
        out_shape=out_shape,
        scratch_shapes=[pltpu.VMEM((R + HALO, pg), BF16), pltpu.VMEM((R + HALO, N), BF16),
                        pltpu.VMEM((R + HALO, N), BF16), pltpu.VMEM((N, pg), F32)],
        compiler_params=_cparams("parallel", "parallel", "arbitrary"),
        name="ssd_scan",
    )(big, big, big, big, dtg, conv_w, conv_w, conv_w, conv_b, conv_b, conv_b,
      head_params, dskip_x, shift, spread, halo_x, halo_b, halo_c, state0)


def _outproj_kernel(attn_ref, y_ref, x_ref, ga_ref, gs_ref, w_ref, gf_ref, wr_ref, br_ref,
                    h1_ref, n2_ref, logit_ref, *, attn_width):
    def norm(v, g):
        return v * lax.rsqrt(jnp.mean(v * v, axis=-1, keepdims=True) + EPS) * g

    an = norm(attn_ref[...].astype(F32), ga_ref[...]).astype(BF16)
    yn = norm(y_ref[...], gs_ref[...]).astype(BF16)
    mixed = (jnp.dot(an, w_ref[0:attn_width, :], preferred_element_type=F32)
             + jnp.dot(yn, w_ref[attn_width:, :], preferred_element_type=F32))
    h1 = x_ref[...] + mixed
    n2 = norm(h1, gf_ref[...])
    h1_ref[...] = h1
    half = n2.shape[1] // 2
    n2_ref[...] = _pack_bf16_pair(n2[:, :half], n2[:, half:])
    logit_ref[...] = jnp.dot(n2.astype(BF16), wr_ref[...], preferred_element_type=F32) + br_ref[...]


def _outproj(attn, yssm, x, g_attn, g_ssm, w_out, g_ffn, w_router, b_router, tm=512):
    m, d = x.shape
    aw, sw = attn.shape[1], yssm.shape[1]
    tm = min(tm, m)
    row = lambda w: pl.BlockSpec((tm, w), lambda i: (i, 0))
    full = lambda r, c: pl.BlockSpec((r, c), lambda i: (0, 0))
    return pl.pallas_call(
        functools.partial(_outproj_kernel, attn_width=aw),
        grid=(m // tm,),
        in_specs=[row(aw), row(sw), row(d), full(1, aw), full(1, sw), full(aw + sw, d), full(1, d),
                  full(d, LANES), full(1, LANES)],
        out_specs=[row(d), row(d // 2), row(LANES)],
        out_shape=[jax.ShapeDtypeStruct((m, d), F32), jax.ShapeDtypeStruct((m, d // 2), U32),
                   jax.ShapeDtypeStruct((m, LANES), F32)],
        compiler_params=_cparams("parallel"),
        name="outproj_router",
    )(attn, yssm, x, g_attn.reshape(1, aw), g_ssm.reshape(1, sw), w_out, g_ffn.reshape(1, d),
      w_router, b_router)


def _route_kernel(logit_ref, idx_ref, gate_ref, cnt_ref, carry_ref, *, tm):
    i = pl.program_id(0)

    @pl.when(i == 0)
    def _():
        carry_ref[...] = jnp.zeros_like(carry_ref)

    work = logit_ref[...]
    lane = lax.broadcasted_iota(I32, work.shape, 1)
    lane_f = lane.astype(F32)
    vals, ids, hots = [], [], []
    for _ in range(TOP_K):
        mk = jnp.max(work, axis=-1, keepdims=True)
        ik = jnp.min(jnp.where(work == mk, lane_f, float(LANES)), axis=-1, keepdims=True)
        hot = lane_f == ik
        work = jnp.where(hot, -jnp.inf, work)
        vals.append(mk)
        ids.append(ik)
        hots.append(hot)
    exps = [jnp.exp(v - vals[0]) for v in vals]
    denom = exps[0] + exps[1] + exps[2] + exps[3]
    sel = jnp.zeros(work.shape, F32)
    for hot in hots:
        sel = sel + hot.astype(F32)
    strict = (lax.broadcasted_iota(I32, (tm, tm), 1) < lax.broadcasted_iota(I32, (tm, tm), 0)).astype(BF16)
    before = jnp.dot(strict, sel.astype(BF16), preferred_element_type=F32) + carry_ref[...]
    idx_out = jnp.zeros(work.shape, I32)
    gate_out = jnp.zeros(work.shape, F32)
    for k in range(TOP_K):
        rank = jnp.sum(jnp.where(hots[k], before, 0.0), axis=-1, keepdims=True).astype(I32)
        idx_out = jnp.where(lane == k, ids[k].astype(I32), idx_out)
        idx_out = jnp.where(lane == TOP_K + k, rank, idx_out)
        gate_out = jnp.where(lane == k, exps[k] / denom, gate_out)
    idx_ref[...] = idx_out
    gate_ref[...] = gate_out
    carry_ref[...] = carry_ref[...] + jnp.sum(sel, axis=0, keepdims=True)
    cnt_ref[...] = carry_ref[...]


def _route(logits, tm=512):
    m = logits.shape[0]
    tm = min(tm, m)
    return pl.pallas_call(
        functools.partial(_route_kernel, tm=tm),
        grid=(m // tm,),
        in_specs=[pl.BlockSpec((tm, LANES), lambda i: (i, 0))],
        out_specs=[pl.BlockSpec((tm, LANES), lambda i: (i, 0)), pl.BlockSpec((tm, LANES), lambda i: (i, 0)),
                   pl.BlockSpec((1, LANES), lambda i: (0, 0))],
        out_shape=[jax.ShapeDtypeStruct((m, LANES), I32), jax.ShapeDtypeStruct((m, LANES), F32),
                   jax.ShapeDtypeStruct((1, LANES), F32)],
        scratch_shapes=[pltpu.VMEM((1, LANES), F32)],
        compiler_params=_cparams("arbitrary"),
        name="route_top4",
    )(logits)


def _dispatch_kernel(dest_ref, src_ref, init_ref, out_ref, sem, *, td):
    del init_ref

    def issue(j, c):
        for k in range(TOP_K):
            pltpu.make_async_copy(src_ref.at[pl.ds(j, 1)], out_ref.at[pl.ds(dest_ref[j * TOP_K + k], 1)],
                                  sem).start(priority=k % 2)
        return c

    lax.fori_loop(0, td, issue, 0, unroll=8)
    for _ in range(TOP_K):
        pltpu.make_async_copy(src_ref, out_ref.at[pl.ds(0, td)], sem).wait()


def _dispatch(dest_flat, src, n_rows_out, td=512):
    m, d = src.shape
    td = min(td, m)
    init = jnp.zeros((n_rows_out, d), src.dtype)
    return pl.pallas_call(
        functools.partial(_dispatch_kernel, td=td),
        grid=(m // td,),
        in_specs=[pl.BlockSpec((td * TOP_K,), lambda i: (i,), memory_space=pltpu.SMEM),
                  pl.BlockSpec((td, d), lambda i: (i, 0)), pl.BlockSpec(memory_space=pl.ANY)],
        out_specs=pl.BlockSpec(memory_space=pl.ANY),
        out_shape=jax.ShapeDtypeStruct((n_rows_out, d), src.dtype),
        scratch_shapes=[pltpu.SemaphoreType.DMA],
        input_output_aliases={2: 0},
        compiler_params=_cparams("arbitrary"),
        name="moe_dispatch",
    )(dest_flat, src, init)


def _grouped_kernel(te_ref, first_ref, nxt_ref, cidx_ref, valid_ref, cnt_ref, x_hbm, w_hbm, b_ref, o_hbm,
                    xbuf, wstage, wbf, obuf, zbuf, x_sem, w_sem, o_sem, z_sem,
                    *, tm, tn, to, n_sweeps, n_pieces, n_tiles, tile_fn, zero_fn):
    n = pl.program_id(0)
    n_used, n_chg = cnt_ref[0], cnt_ref[1]

    def x_copy(m, slot):
        return pltpu.make_async_copy(x_hbm.at[pl.ds(m * tm, tm)], xbuf.at[slot], x_sem.at[slot])

    def w_copy(e, sweep, slot, p):
        col = pl.multiple_of((p * n_sweeps + sweep) * tn, tn)
        return pltpu.make_async_copy(w_hbm.at[e, :, pl.ds(col, tn)], wstage.at[slot, p], w_sem.at[slot])

    def o_copy(m, slot):
        return pltpu.make_async_copy(obuf.at[slot], o_hbm.at[pl.ds(m * tm, tm), pl.ds(n * to, to)], o_sem.at[slot])

    def z_copy(m):
        return pltpu.make_async_copy(zbuf, o_hbm.at[pl.ds(m * tm, tm), pl.ds(n * to, to)], z_sem)

    @pl.when(n == 0)
    def _():
        zbuf[...] = zero_fn(zbuf.shape)
        x_copy(0, 0).start()
        for p in range(n_pieces):
            w_copy(te_ref[0], 0, 0, p).start()

    def tile(m, carry):
        gm = n * n_used + m
        slot = lax.rem(gm, 2)

        @pl.when(first_ref[m] == 1)
        def _():
            ws = lax.rem(n * n_chg + cidx_ref[m], 2)
            for p in range(n_pieces):
                w_copy(0, 0, ws, p).wait()
            for p in range(n_pieces):
                wbf[p] = wstage[ws, p].astype(BF16)
            nm = nxt_ref[m]

            @pl.when(nm >= 0)
            def _():
                for p in range(n_pieces):
                    w_copy(te_ref[jnp.maximum(nm, 0)], n, 1 - ws, p).start()

            @pl.when(jnp.logical_and(nm < 0, n + 1 < n_sweeps))
            def _():
                for p in range(n_pieces):
                    w_copy(te_ref[0], n + 1, 1 - ws, p).start()

        x_copy(0, slot).wait()

        @pl.when(m + 1 < n_used)
        def _():
            x_copy(m + 1, 1 - slot).start()

        @pl.when(jnp.logical_and(m + 1 >= n_used, n + 1 < n_sweeps))
        def _():
            x_copy(0, 1 - slot).start()

        e = te_ref[m]
        biases = [b_ref[pl.ds(e, 1), pl.ds(pl.multiple_of((p * n_sweeps + n) * tn, tn), tn)]
                  for p in range(n_pieces)]

        @pl.when(gm >= 2)
        def _():
            o_copy(0, slot).wait()

        half = tm // 2

        @pl.when(valid_ref[m] > half)
        def _():
            obuf[slot] = tile_fn(xbuf[slot], wbf, biases)

        @pl.when(valid_ref[m] <= half)
        def _():
            obuf[slot, 0:half] = tile_fn(xbuf[slot, 0:half], wbf, biases)
            obuf[slot, half:tm] = zero_fn((tm - half, to))

        o_copy(m, slot).start()
        return carry

    lax.fori_loop(0, n_used, tile, 0)

    def zero_tail(m, carry):
        z_copy(m).start()
        return carry

    def zero_tail_wait(m, carry):
        z_copy(0).wait()
        return carry

    lax.fori_loop(n_used, n_tiles, zero_tail, 0)
    lax.fori_loop(n_used, n_tiles, zero_tail_wait, 0)

    @pl.when(n == n_sweeps - 1)
    def _():
        total = n_sweeps * n_used
        for back in (1, 2):
            @pl.when(total >= back)
            def _():
                o_copy(0, lax.rem(total - back, 2)).wait()


def _gate_up_tile(x, wbf, biases):
    lo, hi = _unpack_bf16_pair(x)
    lo, hi = lo.astype(BF16), hi.astype(BF16)
    half = lo.shape[1]

    def proj(p):
        return (jnp.dot(lo, wbf[p, 0:half, :], preferred_element_type=F32)
                + jnp.dot(hi, wbf[p, half:, :], preferred_element_type=F32) + biases[p])

    g = jnp.minimum(proj(0), SWIGLU_LIMIT)
    u = jnp.clip(proj(1), -SWIGLU_LIMIT, SWIGLU_LIMIT)
    return (g * _sigmoid(SWIGLU_ALPHA * g) * (u + 1.0)).astype(BF16)


def _down_tile(h, wbf, biases):
    y = jnp.dot(h, wbf[0], preferred_element_type=F32) + biases[0]
    half = y.shape[1] // 2
    return _pack_bf16_pair(y[:, :half], y[:, half:])


def _packed_zeros(shape):
    zero = jnp.zeros(shape, F32)
    return _pack_bf16_pair(zero, zero)


def _grouped_matmul(plan, x, w, bias, out_dtype, *, tm, tn, to, n_sweeps, n_pieces, tile_fn, zero_fn, name):
    rows, kx = x.shape
    n_tiles = rows // tm
    k = w.shape[1]
    kern = functools.partial(_grouped_kernel, tm=tm, tn=tn, to=to, n_sweeps=n_sweeps, n_pieces=n_pieces,
                             n_tiles=n_tiles, tile_fn=tile_fn, zero_fn=zero_fn)
    any_spec = pl.BlockSpec(memory_space=pl.ANY)
    grid_spec = pltpu.PrefetchScalarGridSpec(
        num_scalar_prefetch=6,
        grid=(n_sweeps,),
        in_specs=[any_spec, any_spec, pl.BlockSpec(bias.shape, lambda n, *_: (0, 0))],
        out_specs=any_spec,
        scratch_shapes=[pltpu.VMEM((2, tm, kx), x.dtype), pltpu.VMEM((2, n_pieces, k, tn), F32),
                        pltpu.VMEM((n_pieces, k, tn), BF16), pltpu.VMEM((2, tm, to), out_dtype),
                        pltpu.VMEM((tm, to), out_dtype), pltpu.SemaphoreType.DMA((2,)),
                        pltpu.SemaphoreType.DMA((2,)), pltpu.SemaphoreType.DMA((2,)), pltpu.SemaphoreType.DMA],
    )
    return pl.pallas_call(
        kern,
        grid_spec=grid_spec,
        out_shape=jax.ShapeDtypeStruct((rows, n_sweeps * to), out_dtype),
        compiler_params=_cparams("arbitrary"),
        name=name,
    )(*plan, x, w, bias)


def _grouped_gate_up(plan, xs, w_gu, b_gu, tm, tn=512):
    ff = w_gu.shape[2] // 2
    tn = min(tn, ff)
    return _grouped_matmul(plan, xs, w_gu, b_gu, BF16, tm=tm, tn=tn, to=tn, n_sweeps=ff // tn, n_pieces=2,
                           tile_fn=_gate_up_tile, zero_fn=lambda s: jnp.zeros(s, BF16), name="moe_gate_up")


def _grouped_down(plan, hdn, w_d, b_d, tm, tn=1024):
    d = w_d.shape[2]
    tn = min(tn, d)
    return _grouped_matmul(plan, hdn, w_d, b_d, U32, tm=tm, tn=tn, to=tn // 2, n_sweeps=d // tn, n_pieces=1,
                           tile_fn=_down_tile, zero_fn=_packed_zeros, name="moe_down")


def _combine_kernel(dest_ref, ys_ref, gate_ref, h1_ref, g_ref, o_ref, buf_ref, sem, *, tc, pack_cols):
    def issue(j, c):
        for k in range(TOP_K):
            pltpu.make_async_copy(ys_ref.at[pl.ds(dest_ref[j * TOP_K + k], 1)], buf_ref.at[k, pl.ds(j, 1)],
                                  sem).start(priority=k % 2)
        return c

    lax.fori_loop(0, tc, issue, 0, unroll=8)
    for k in range(TOP_K):
        pltpu.make_async_copy(ys_ref.at[pl.ds(0, tc)], buf_ref.at[k], sem).wait()

    gates = gate_ref[...]
    hw = pack_cols // 2
    cols = []
    for n in range(buf_ref.shape[2] // hw):
        lo_acc = h1_ref[:, n * pack_cols:n * pack_cols + hw]
        hi_acc = h1_ref[:, n * pack_cols + hw:(n + 1) * pack_cols]
        for k in range(TOP_K):
            lo, hi = _unpack_bf16_pair(buf_ref[k, :, n * hw:(n + 1) * hw])
            lo_acc = lo_acc + gates[:, k:k + 1] * lo
            hi_acc = hi_acc + gates[:, k:k + 1] * hi
        cols += [lo_acc, hi_acc]
    h2 = jnp.concatenate(cols, axis=1)
    o_ref[...] = h2 * lax.rsqrt(jnp.mean(h2 * h2, axis=-1, keepdims=True) + EPS) * g_ref[...]


def _combine(dest_flat, ys, gates, h1, g_final, pack_cols, tc=512):
    m, d = h1.shape
    tc = min(tc, m)
    return pl.pallas_call(
        functools.partial(_combine_kernel, tc=tc, pack_cols=pack_cols),
        grid=(m // tc,),
        in_specs=[pl.BlockSpec((tc * TOP_K,), lambda i: (i,), memory_space=pltpu.SMEM),
                  pl.BlockSpec(memory_space=pl.ANY),
                  pl.BlockSpec((tc, LANES), lambda i: (i, 0)),
                  pl.BlockSpec((tc, d), lambda i: (i, 0)),
                  pl.BlockSpec((1, d), lambda i: (0, 0))],
        out_specs=pl.BlockSpec((tc, d), lambda i: (i, 0)),
        out_shape=jax.ShapeDtypeStruct((m, d), F32),
        scratch_shapes=[pltpu.VMEM((TOP_K, tc, d // 2), U32), pltpu.SemaphoreType.DMA],
        compiler_params=_cparams("arbitrary"),
        name="moe_combine",
    )(dest_flat, ys, gates, h1, g_final.reshape(1, d))


def _lane_pad(v, width=LANES):
    return jnp.pad(v, ((0, 0), (0, width - v.shape[1])))


def kernel(x, meta_tokens, norm_mix_g, w_in, fgate_b, attn_norm_g, conv_w, conv_b, dt_bias, a_log, d_skip,
           ssm_norm_g, w_out, norm_ffn_g, w_router, b_router, w_gate_up, b_gate_up, w_down, b_down,
           final_norm_g):
    n_batch, seq, d = x.shape
    aw = (d // 128) * HEAD_DIM
    n_ah = aw // HEAD_DIM
    sw = d
    n_sh = sw // HEAD_DIM
    hpg = n_sh // SSM_GROUPS
    gn = SSM_GROUPS * SSM_STATE
    tokens = n_batch * seq
    lyr = 0

    o_f = 3 * aw
    o_z = o_f + n_ah
    o_xbc = o_z + sw
    o_dt = o_xbc + sw + 2 * gn
    wt = jnp.swapaxes(w_in[lyr], 0, 1)
    w_q = wt[:aw] * (LOG2E * HEAD_DIM ** -0.5)
    w_big = jnp.concatenate([w_q, wt[aw:o_f], wt[o_z:o_dt]], axis=0).astype(BF16)
    w_dt = jnp.tile(wt[o_dt:o_dt + n_sh].reshape(SSM_GROUPS, hpg, d), (1, HEAD_COPIES, 1))
    w_dt = jnp.pad(w_dt, ((0, 0), (0, LANES - HEAD_COPIES * hpg), (0, 0))).reshape(SSM_GROUPS * LANES, d)
    w_f = jnp.pad(wt[o_f:o_z], ((0, LANES - n_ah), (0, 0)))
    w_small = jnp.concatenate([w_f, w_dt], axis=0).astype(BF16)
    col_k, col_v, col_z, col_x = aw, 2 * aw, 3 * aw, 3 * aw + sw

    meta_rows = jnp.concatenate([jnp.zeros((CHUNK - N_META, d), F32), meta_tokens.astype(F32)], axis=0)
    g_mix = norm_mix_g[lyr]
    n_meta = _rmsnorm_rows(meta_rows, g_mix, CHUNK)
    n_real = _rmsnorm_rows(x.reshape(tokens, d), g_mix, 512)
    big_meta = _matmul(n_meta, w_big, BF16, CHUNK, 1024, "inproj_meta")
    small_meta = _matmul(n_meta, w_small, F32, CHUNK, 640, "inproj_small_meta")
    big = _matmul(n_real, w_big, BF16, 1024, 1024, "inproj")
    small = _matmul(n_real, w_small, F32, 1024, 640, "inproj_small")

    fb = _lane_pad(fgate_b[lyr].reshape(1, n_ah).astype(F32))
    lane_maps = _gate_lane_maps(n_ah)
    xk_meta, _, c_meta_last = _fgate_cumsum(small_meta[:, :LANES], fb, jnp.zeros((1, LANES), F32), lane_maps, 1,
                                            n_lead_pad=CHUNK - N_META)
    xk, yq, _ = _fgate_cumsum(small[:, :LANES], fb, c_meta_last[0], lane_maps, n_batch)
    attn = _attention(big, big_meta, xk, yq, xk_meta, n_batch, seq, n_ah // 2, 0, col_k // LANES,
                      col_v // LANES)

    head_params = jnp.zeros((SSM_GROUPS, 8, LANES), F32)
    per_group = lambda v: jnp.tile(v.reshape(SSM_GROUPS, hpg).astype(F32), (1, HEAD_COPIES))
    head_params = head_params.at[:, 0, :HEAD_COPIES * hpg].set(per_group(dt_bias[lyr]))
    head_params = head_params.at[:, 1, :HEAD_COPIES * hpg].set(per_group(a_log[lyr]))
    dskip_x = jnp.repeat(d_skip[lyr].astype(F32), HEAD_DIM).reshape(1, sw)
    cw, cb = conv_w[lyr].astype(F32), conv_b[lyr].astype(F32).reshape(1, -1)
    dtg_meta, dtg = small_meta[:, LANES:], small[:, LANES:]
    pg = sw // SSM_GROUPS
    zeros_h = lambda w: jnp.zeros((1, HALO, w), BF16)
    state0 = jnp.zeros((1, SSM_GROUPS, SSM_STATE, pg), F32)
    _, s_meta, hx, hb, hc = _ssd(big_meta, dtg_meta, cw, cb, head_params, dskip_x, zeros_h(sw), zeros_h(gn),
                                 zeros_h(gn), state0, 1, CHUNK, sw, col_z, col_x, n_lead_pad=CHUNK - N_META)
    yssm = _ssd(big, dtg, cw, cb, head_params, dskip_x, hx, hb, hc, s_meta, n_batch, seq, sw, col_z, col_x)[0]

    wr = _lane_pad(w_router[lyr]).astype(BF16)
    br = jnp.concatenate([b_router[lyr].astype(F32), jnp.full((LANES - N_EXPERTS,), NEG, F32)]).reshape(1, LANES)
    h1, n2, logits = _outproj(attn, yssm, x.reshape(tokens, d), attn_norm_g[lyr], ssm_norm_g[lyr],
                              w_out[lyr].astype(BF16), norm_ffn_g[lyr], wr, br)

    idx, gates, counts = _route(logits)
    tm = min(512, tokens)
    eid, rank = idx[:, :TOP_K], idx[:, TOP_K:2 * TOP_K]
    cnt = counts[0, :N_EXPERTS].astype(I32)
    padded = ((cnt + tm - 1) // tm) * tm
    ends = jnp.cumsum(padded)
    starts = ends - padded
    dest = (starts[eid] + rank).reshape(-1)
    n_tiles = (tokens * TOP_K + N_EXPERTS * (tm - 1)) // tm
    tile_row0 = jnp.arange(n_tiles, dtype=I32) * tm
    tile_expert = jnp.minimum(jnp.sum((ends[None, :] <= tile_row0[:, None]).astype(I32), axis=1), N_EXPERTS - 1)
    n_used = (ends[-1] // tm).astype(I32)
    tile_id = jnp.arange(n_tiles, dtype=I32)
    first = (jnp.concatenate([jnp.ones((1,), bool), tile_expert[1:] != tile_expert[:-1]])
             & (tile_id < n_used)).astype(I32)
    cidx = jnp.cumsum(first) - 1
    later_first = (tile_id[None, :] > tile_id[:, None]) & (first[None, :] == 1)
    nxt = jnp.min(jnp.where(later_first, tile_id[None, :], n_tiles), axis=1)
    nxt = jnp.where(nxt >= n_tiles, -1, nxt).astype(I32)
    valid = jnp.clip(cnt[tile_expert] - (tile_id * tm - starts[tile_expert]), 0, tm)
    valid = jnp.where(tile_id < n_used, valid, 0).astype(I32)
    plan = (tile_expert, first, nxt, cidx.astype(I32), valid, jnp.stack([n_used, jnp.sum(first)]).astype(I32))

    xs_sorted = _dispatch(dest, n2, n_tiles * tm)
    hdn = _grouped_gate_up(plan, xs_sorted, w_gate_up[lyr], b_gate_up[lyr].astype(F32), tm)
    tn_down = min(1024, d)
    ys = _grouped_down(plan, hdn, w_down[lyr], b_down[lyr].astype(F32), tm, tn=tn_down)
    out = _combine(dest, ys, gates, h1, final_norm_g, tn_down)
    return out.reshape(n_batch, seq, d)
```

```python
import functools

import numpy as np
import jax
import jax.numpy as jnp
from jax import lax
from jax.experimental import pallas as pl
from jax.experimental.pallas import tpu as pltpu

F32 = jnp.float32
BF16 = jnp.bfloat16
I32 = jnp.int32
U32 = jnp.uint32

N_META = 16
CHUNK = 128
HEAD_DIM = 64
SSM_GROUPS = 4
SSM_STATE = 128
CONV_WIDTH = 4
N_EXPERTS = 32
TOP_K = 4
SWIGLU_LIMIT = 7.0
SWIGLU_ALPHA = 1.702
EPS = 1e-5
LANES = 128
HALO = 16
NEG = -1e30
LOG2E = 1.4426950408889634
N_SPLIT = 3
HEAD_COPIES = 4
VMEM_LIMIT_BYTES = 56 * 1024 * 1024


def _cparams(*sem):
    return pltpu.CompilerParams(dimension_semantics=sem, vmem_limit_bytes=VMEM_LIMIT_BYTES)


def _sigmoid(x):
    return 0.5 * jnp.tanh(0.5 * x) + 0.5


def _softplus(x):
    return jnp.maximum(x, 0.0) + jnp.log(1.0 + jnp.exp(-jnp.abs(x)))


def _pack_bf16_pair(lo, hi):
    return pltpu.pack_elementwise([lo, hi], packed_dtype=BF16)


def _unpack_bf16_pair(w):
    lo = pltpu.unpack_elementwise(w, index=0, packed_dtype=BF16, unpacked_dtype=F32)
    hi = pltpu.unpack_elementwise(w, index=1, packed_dtype=BF16, unpacked_dtype=F32)
    return lo, hi


def _split_bf16(x, n=2):
    parts = []
    for _ in range(n):
        p = x.astype(BF16)
        parts.append(p)
        x = x - p.astype(F32)
    return parts


def _rmsnorm_kernel(x_ref, g_ref, o_ref):
    x = x_ref[...]
    ms = jnp.mean(x * x, axis=-1, keepdims=True)
    o_ref[...] = (x * lax.rsqrt(ms + EPS) * g_ref[...]).astype(o_ref.dtype)


def _rmsnorm_rows(x, g, tm):
    m, d = x.shape
    tm = min(tm, m)
    return pl.pallas_call(
        _rmsnorm_kernel,
        grid=(m // tm,),
        in_specs=[pl.BlockSpec((tm, d), lambda i: (i, 0)), pl.BlockSpec((1, d), lambda i: (0, 0))],
        out_specs=pl.BlockSpec((tm, d), lambda i: (i, 0)),
        out_shape=jax.ShapeDtypeStruct((m, d), BF16),
        compiler_params=_cparams("parallel"),
        name="rmsnorm_rows",
    )(x, g.reshape(1, d))


def _matmul_kernel(a_ref, wt_ref, o_ref):
    o_ref[...] = lax.dot_general(a_ref[...], wt_ref[...], (((1,), (1,)), ((), ())),
                                 preferred_element_type=F32).astype(o_ref.dtype)


def _pick_tile(n, pref, unit=LANES):
    best = unit
    for t in range(unit, min(pref, n) + 1, unit):
        if n % t == 0:
            best = t
    return best


def _matmul(a, wt, out_dtype, tm, tn, name):
    m, k = a.shape
    n = wt.shape[0]
    tm, tn = min(tm, m), _pick_tile(n, tn)
    return pl.pallas_call(
        _matmul_kernel,
        grid=(n // tn, m // tm),
        in_specs=[pl.BlockSpec((tm, k), lambda j, i: (i, 0)), pl.BlockSpec((tn, k), lambda j, i: (j, 0))],
        out_specs=pl.BlockSpec((tm, tn), lambda j, i: (i, j)),
        out_shape=jax.ShapeDtypeStruct((m, n), out_dtype),
        compiler_params=_cparams("parallel", "parallel"),
        name=name,
    )(a, wt)


def _gate_lane_maps(n_heads):
    n_pairs = n_heads // 2
    pk = np.zeros((N_SPLIT * LANES, n_pairs * LANES), np.float32)
    pq = np.zeros_like(pk)
    ones_k = np.zeros((1, n_pairs * LANES), np.float32)
    ones_q = np.zeros_like(ones_k)
    for h in range(n_heads):
        base = (h // 2) * LANES + (HEAD_DIM if h % 2 == 0 else 0)
        for i in range(N_SPLIT):
            pk[i * LANES + h, base + i] = -1.0
            pq[i * LANES + h, base + N_SPLIT + i] = 1.0
            ones_q[0, base + i] = 1.0
            ones_k[0, base + N_SPLIT + i] = 1.0
    return (jnp.asarray(pk, BF16), jnp.asarray(pq, BF16), jnp.asarray(ones_k), jnp.asarray(ones_q))


def _fgate_cumsum_kernel(f_ref, b_ref, init_ref, pk_ref, pq_ref, ok_ref, oq_ref, xk_ref, yq_ref, last_ref,
                         carry_ref, *, blk, n_lead_pad):
    j = pl.program_id(1)

    @pl.when(j == 0)
    def _():
        carry_ref[...] = init_ref[...]

    x = f_ref[...] + b_ref[...]
    lf = jnp.minimum(x, 0.0) - jnp.log(1.0 + jnp.exp(-jnp.abs(x)))
    if n_lead_pad:
        row = lax.broadcasted_iota(I32, lf.shape, 0) + j * blk
        lf = jnp.where(row < n_lead_pad, 0.0, lf)
    hi, lo = _split_bf16(lf)
    lower = (lax.broadcasted_iota(I32, (blk, blk), 0) >= lax.broadcasted_iota(I32, (blk, blk), 1)).astype(BF16)
    c = (jnp.dot(lower, hi, preferred_element_type=F32) + jnp.dot(lower, lo, preferred_element_type=F32)
         + carry_ref[...])
    cc = jnp.concatenate(_split_bf16(c * LOG2E, N_SPLIT), axis=1)
    xk_ref[...] = (jnp.dot(cc, pk_ref[...], preferred_element_type=F32) + ok_ref[...]).astype(BF16)
    yq_ref[...] = (jnp.dot(cc, pq_ref[...], preferred_element_type=F32) + oq_ref[...]).astype(BF16)
    carry_ref[...] = c[blk - 1:blk, :]
    last_ref[0] = c[blk - 1:blk, :]


def _fgate_cumsum(f_small, bias_row, init_row, lane_maps, n_batch, n_lead_pad=0, blk=512):
    rows = f_small.shape[0]
    s = rows // n_batch
    blk = min(blk, s)
    nb = s // blk
    pk, pq, ones_k, ones_q = lane_maps
    width = pk.shape[1]
    const = lambda a: pl.BlockSpec(a.shape, lambda b, j: (0, 0))
    return pl.pallas_call(
        functools.partial(_fgate_cumsum_kernel, blk=blk, n_lead_pad=n_lead_pad),
        grid=(n_batch, nb),
        in_specs=[pl.BlockSpec((blk, LANES), lambda b, j: (b * nb + j, 0)),
                  pl.BlockSpec((1, LANES), lambda b, j: (0, 0)),
                  pl.BlockSpec((1, LANES), lambda b, j: (0, 0)),
                  const(pk), const(pq), const(ones_k), const(ones_q)],
        out_specs=[pl.BlockSpec((blk, width), lambda b, j: (b * nb + j, 0)),
                   pl.BlockSpec((blk, width), lambda b, j: (b * nb + j, 0)),
                   pl.BlockSpec((1, 1, LANES), lambda b, j: (b, 0, 0))],
        out_shape=[jax.ShapeDtypeStruct((rows, width), BF16), jax.ShapeDtypeStruct((rows, width), BF16),
                   jax.ShapeDtypeStruct((n_batch, 1, LANES), F32)],
        scratch_shapes=[pltpu.VMEM((1, LANES), F32)],
        compiler_params=_cparams("parallel", "arbitrary"),
        name="fgate_cumsum",
    )(f_small, bias_row, init_row, pk, pq, ones_k, ones_q)


def _attn_kernel(q_ref, k_ref, v_ref, xk_ref, yq_ref, km_ref, vm_ref, xkm_ref, o_ref,
                 kx_ref, vt_ref, m_ref, acc_ref, sa_ref, sb_ref, *, tq, tk, n_q, n_meta_pad):
    lo_half = lax.broadcasted_iota(I32, (tq, LANES), 1) < HEAD_DIM
    krow = lax.broadcasted_iota(I32, (tk, tq), 0)
    qcol = lax.broadcasted_iota(I32, (tk, tq), 1)
    meta_valid = lax.broadcasted_iota(I32, (LANES, tq), 0) >= n_meta_pad
    top_half = lax.broadcasted_iota(I32, (LANES, tq), 0) < HEAD_DIM
    nt = (((1,), (1,)), ((), ()))

    def lo_mask(rows):
        return lax.broadcasted_iota(I32, (rows, LANES), 1) < HEAD_DIM

    def with_ones_t(vals, lo, first):
        ones = jnp.ones(vals.shape, F32)
        vf = vals.astype(F32)
        return (jnp.where(lo, vf, ones) if first else jnp.where(lo, ones, vf)).T.astype(BF16)

    k, v, xk = k_ref[...], v_ref[...], xk_ref[...]
    lo_seq = lo_mask(k.shape[0])
    kx_ref[0] = jnp.where(lo_seq, k, xk)
    kx_ref[1] = jnp.where(lo_seq, xk, k)
    vt_ref[0] = with_ones_t(v, lo_seq, True)
    vt_ref[1] = with_ones_t(v, lo_seq, False)
    km, vm, xkm = km_ref[...], vm_ref[...], xkm_ref[...]
    lo_meta = lo_mask(km.shape[0])
    kmx = (jnp.where(lo_meta, km, xkm), jnp.where(lo_meta, xkm, km))
    vmt = (with_ones_t(vm, lo_meta, True), with_ones_t(vm, lo_meta, False))

    def q_block(qi, carry):
        q0 = pl.multiple_of(qi * tq, tq)
        q = q_ref[pl.ds(q0, tq), :]
        yq = yq_ref[pl.ds(q0, tq), :]
        qx = (jnp.where(lo_half, q, yq), jnp.where(lo_half, yq, q))

        def scores(dst_ref, k0):
            for h in range(2):
                dst_ref[h] = lax.dot_general(kx_ref[h, pl.ds(k0, tk), :], qx[h], nt, preferred_element_type=F32)

        def softmax_pv(src_ref, k0, mask):
            for h in range(2):
                t = src_ref[h] if mask is None else jnp.where(mask, src_ref[h], NEG)
                m_prev = m_ref[h]
                m_new = jnp.maximum(m_prev, jnp.max(t, axis=0, keepdims=True))
                p = jnp.exp2(t - m_new).astype(BF16)
                acc_ref[h] = (jnp.exp2(m_prev - m_new) * acc_ref[h]
                              + jnp.dot(vt_ref[h, :, pl.ds(k0, tk)], p, preferred_element_type=F32))
                m_ref[h] = m_new

        block = lambda b: pl.multiple_of(b * tk, tk)
        causal = krow <= qcol
        scores(sa_ref, block(0))

        for h in range(2):
            t = jnp.where(meta_valid, lax.dot_general(kmx[h], qx[h], nt, preferred_element_type=F32), NEG)
            m0 = jnp.max(t, axis=0, keepdims=True)
            m_ref[h] = m0
            acc_ref[h] = jnp.dot(vmt[h], jnp.exp2(t - m0).astype(BF16), preferred_element_type=F32)

        def block_pair(p, c):
            scores(sb_ref, block(2 * p + 1))
            softmax_pv(sa_ref, block(2 * p), None)
            scores(sa_ref, block(2 * p + 2))
            softmax_pv(sb_ref, block(2 * p + 1), None)
            return c

        lax.fori_loop(0, qi // 2, block_pair, 0)

        @pl.when(qi % 2 == 1)
        def _():
            scores(sb_ref, q0)
            softmax_pv(sa_ref, block(qi - 1), None)
            softmax_pv(sb_ref, q0, causal)

        @pl.when(qi % 2 == 0)
        def _():
            softmax_pv(sa_ref, q0, causal)

        a0, a1 = acc_ref[0], acc_ref[1]
        out_t = jnp.where(top_half, a0 / a0[HEAD_DIM:HEAD_DIM + 1, :], a1 / a1[0:1, :])
        o_ref[pl.ds(q0, tq), :] = out_t.T.astype(o_ref.dtype)
        return carry

    lax.fori_loop(0, n_q, q_block, 0)


def _attention(big, big_meta, xk, yq, xk_meta, n_batch, seq, n_pairs, col_q, col_k, col_v, tq=512, tk=512):
    tq = min(tq, seq)
    tk = min(tk, tq)
    n_q = seq // tq
    mrows = big_meta.shape[0]
    assert tk == tq, "the causal block is a single (tk, tq) block"
    kern = functools.partial(_attn_kernel, tq=tq, tk=tk, n_q=n_q, n_meta_pad=mrows - N_META)
    blk = lambda c0: pl.BlockSpec((seq, LANES), lambda b, p, c0=c0: (b, c0 + p))
    mblk = lambda c0: pl.BlockSpec((mrows, LANES), lambda b, p, c0=c0: (0, c0 + p))
    return pl.pallas_call(
        kern,
        grid=(n_batch, n_pairs),
        in_specs=[blk(col_q), blk(col_k), blk(col_v), blk(0), blk(0), mblk(col_k), mblk(col_v), mblk(0)],
        out_specs=pl.BlockSpec((seq, LANES), lambda b, p: (b, p)),
        out_shape=jax.ShapeDtypeStruct((n_batch * seq, n_pairs * LANES), BF16),
        scratch_shapes=[pltpu.VMEM((2, seq, LANES), BF16), pltpu.VMEM((2, LANES, seq), BF16),
                        pltpu.VMEM((2, 1, tq), F32), pltpu.VMEM((2, LANES, tq), F32),
                        pltpu.VMEM((2, tk, tq), F32), pltpu.VMEM((2, tk, tq), F32)],
        compiler_params=_cparams("parallel", "parallel"),
        name="fox_attention",
    )(big, big, big, xk, yq, big_meta, big_meta, xk_meta)


def _ssd_kernel(x_ref, b_ref, c_ref, z_ref, dt_ref, wx_ref, wb_ref, wc_ref, bx_ref, bb_ref, bc_ref,
                hp_ref, dsk_ref, shift_ref, spread_ref, hx0_ref, hb0_ref, hc0_ref, s0_ref,
                y_ref, sout_ref, hxo_ref, hbo_ref, hco_ref,
                xpad_ref, bpad_ref, cpad_ref, state_ref, *, n_pairs, n_sub, n_lead_pad):
    c = pl.program_id(2)
    n_c = pl.num_programs(2)
    L = CHUNK
    R = n_sub * L
    pg, hpg = n_pairs * LANES, 2 * n_pairs

    @pl.when(c == 0)
    def _():
        xpad_ref[0:HALO, :] = hx0_ref[0]
        bpad_ref[0:HALO, :] = hb0_ref[0]
        cpad_ref[0:HALO, :] = hc0_ref[0]
        state_ref[...] = s0_ref[0, 0]

    n_shift = CONV_WIDTH - 1
    xpad_ref[HALO:HALO + R, :] = x_ref[...]
    bpad_ref[HALO:HALO + R, :] = b_ref[...]
    cpad_ref[HALO:HALO + R, :] = c_ref[...]

    def conv_silu(pad_ref, w_ref, bias_ref, r0):
        taps = jnp.dot(shift_ref[...], pad_ref[r0:r0 + L + HALO, :], preferred_element_type=F32)
        acc = bias_ref[...] + w_ref[n_shift:n_shift + 1, :] * pad_ref[r0 + HALO:r0 + HALO + L, :].astype(F32)
        for kk in range(n_shift):
            acc = acc + w_ref[kk:kk + 1, :] * taps[kk * L:(kk + 1) * L, :]
        return acc * _sigmoid(acc)

    hp = hp_ref[0]
    neg_a = -jnp.exp(hp[1:2, :])
    ri = lax.broadcasted_iota(I32, (L, L), 0)
    ci = lax.broadcasted_iota(I32, (L, L), 1)
    tri = ri >= ci
    tri_bf = tri.astype(BF16)
    lane = lax.broadcasted_iota(I32, (L, LANES), 1)
    lo_half = lane < HEAD_DIM
    lo_row = lo_half[0:1, :]

    state = state_ref[...]
    for i in range(n_sub):
        r0 = i * L
        xs = conv_silu(xpad_ref, wx_ref, bx_ref, r0)
        bm = conv_silu(bpad_ref, wb_ref, bb_ref, r0)
        cm = conv_silu(cpad_ref, wc_ref, bc_ref, r0)

        dt = _softplus(dt_ref[r0:r0 + L, :] + hp[0:1, :])
        if n_lead_pad:
            row = lax.broadcasted_iota(I32, dt.shape, 0) + (c * R + r0)
            dt = jnp.where(row < n_lead_pad, 0.0, dt)
        a = dt * neg_a
        a_hi, a_lo = _split_bf16(a)
        a_cs = (jnp.dot(tri_bf, a_hi, preferred_element_type=F32)
                + jnp.dot(tri_bf, a_lo, preferred_element_type=F32))
        a_cst = a_cs.T
        a_last = a_cs[L - 1:L, :]
        etot = jnp.exp(a_last)
        cols = jnp.where(lane < hpg, dt,
                         jnp.where(lane < 2 * hpg, jnp.exp(a_cs),
                                   jnp.where(lane < 3 * hpg, jnp.exp(a_last - a_cs), a_cs)))
        spread = jnp.dot(jnp.concatenate(_split_bf16(cols), axis=1), spread_ref[...], preferred_element_type=F32)
        dt_x, ea_x, wdec_x = spread[:, 0:pg], spread[:, pg:2 * pg], spread[:, 2 * pg:3 * pg]

        cm_bf = cm.astype(BF16)
        bm_bf = bm.astype(BF16)
        cb = lax.dot_general(cm_bf, bm_bf, (((1,), (1,)), ((), ())), preferred_element_type=F32)
        y_off = jnp.dot(cm_bf, state.astype(BF16), preferred_element_type=F32)

        xw_parts = []
        etot_parts = []
        for p in range(n_pairs):
            ha, hb = 2 * p, 2 * p + 1
            sl = slice(p * LANES, (p + 1) * LANES)
            xs_p = xs[:, sl]
            xdt = xs_p * dt_x[:, sl]
            xdt_bf = xdt.astype(BF16)
            y_heads = []
            for hh in (ha, hb):
                seg = spread[:, 3 * pg + hh * L:3 * pg + (hh + 1) * L] - a_cst[hh:hh + 1, :]
                dec = jnp.where(tri, jnp.exp(seg), 0.0)
                y_heads.append(jnp.dot((cb * dec).astype(BF16), xdt_bf, preferred_element_type=F32))
            y = jnp.where(lo_half, y_heads[0], y_heads[1]) + y_off[:, sl] * ea_x[:, sl] + dsk_ref[:, sl] * xs_p
            zf = z_ref[r0:r0 + L, sl].astype(F32)
            y_ref[r0:r0 + L, sl] = (y * (zf * _sigmoid(zf))).astype(y_ref.dtype)
            xw_parts.append((xdt * wdec_x[:, sl]).astype(BF16))
            etot_parts.append(jnp.where(lo_row, etot[:, ha:ha + 1], etot[:, hb:hb + 1]))

        xw = xw_parts[0] if n_pairs == 1 else jnp.concatenate(xw_parts, axis=1)
        etot_x = etot_parts[0] if n_pairs == 1 else jnp.concatenate(etot_parts, axis=1)
        state = state * etot_x + jnp.dot(bm.T.astype(BF16), xw, preferred_element_type=F32)
    state_ref[...] = state

    xpad_ref[0:HALO, :] = xpad_ref[R:R + HALO, :]
    bpad_ref[0:HALO, :] = bpad_ref[R:R + HALO, :]
    cpad_ref[0:HALO, :] = cpad_ref[R:R + HALO, :]

    @pl.when(c == n_c - 1)
    def _():
        sout_ref[0, 0] = state_ref[...]
        hxo_ref[0] = xpad_ref[0:HALO, :]
        hbo_ref[0] = bpad_ref[0:HALO, :]
        hco_ref[0] = cpad_ref[0:HALO, :]


def _ssd(big, dtg, conv_w, conv_b, head_params, dskip_x, halo_x, halo_b, halo_c, state0,
         n_batch, seq, d_inner, col_z, col_x, n_lead_pad=0):
    G, N, L = SSM_GROUPS, SSM_STATE, CHUNK
    pg = d_inner // G
    n_pairs = pg // LANES
    n_sub = 2 if seq % (2 * L) == 0 else 1
    R = n_sub * L
    n_c = seq // R
    cx, cz = col_x // pg, col_z // pg
    cb0 = (col_x + d_inner) // N
    cc0 = cb0 + G
    row = lambda b, g, c: b * n_c + c
    kern = functools.partial(_ssd_kernel, n_pairs=n_pairs, n_sub=n_sub, n_lead_pad=n_lead_pad)
    n_shift = CONV_WIDTH - 1
    shift_np = np.zeros((n_shift * L, L + HALO), np.float32)
    for kk in range(n_shift):
        shift_np[kk * L + np.arange(L), np.arange(L) + HALO - n_shift + kk] = 1.0
    shift = jnp.asarray(shift_np, BF16)
    hpg = 2 * n_pairs
    spread_np = np.zeros((2, LANES, 3 * pg + hpg * L), np.float32)
    for h in range(hpg):
        for k in range(3):
            spread_np[:, k * hpg + h, k * pg + h * HEAD_DIM:k * pg + (h + 1) * HEAD_DIM] = 1.0
        spread_np[:, 3 * hpg + h, 3 * pg + h * L:3 * pg + (h + 1) * L] = 1.0
    spread = jnp.asarray(spread_np.reshape(2 * LANES, -1), BF16)
    in_specs = [
        pl.BlockSpec((R, pg), lambda b, g, c: (row(b, g, c), cx + g)),
        pl.BlockSpec((R, N), lambda b, g, c: (row(b, g, c), cb0 + g)),
        pl.BlockSpec((R, N), lambda b, g, c: (row(b, g, c), cc0 + g)),
        pl.BlockSpec((R, pg), lambda b, g, c: (row(b, g, c), cz + g)),
        pl.BlockSpec((R, LANES), lambda b, g, c: (row(b, g, c), g)),
        pl.BlockSpec((CONV_WIDTH, pg), lambda b, g, c: (0, g)),
        pl.BlockSpec((CONV_WIDTH, N), lambda b, g, c: (0, d_inner // N + g)),
        pl.BlockSpec((CONV_WIDTH, N), lambda b, g, c: (0, d_inner // N + G + g)),
        pl.BlockSpec((1, pg), lambda b, g, c: (0, g)),
        pl.BlockSpec((1, N), lambda b, g, c: (0, d_inner // N + g)),
        pl.BlockSpec((1, N), lambda b, g, c: (0, d_inner // N + G + g)),
        pl.BlockSpec((1, 8, LANES), lambda b, g, c: (g, 0, 0)),
        pl.BlockSpec((1, pg), lambda b, g, c: (0, g)),
        pl.BlockSpec(shift.shape, lambda b, g, c: (0, 0)),
        pl.BlockSpec(spread.shape, lambda b, g, c: (0, 0)),
        pl.BlockSpec((1, HALO, pg), lambda b, g, c: (0, 0, g)),
        pl.BlockSpec((1, HALO, N), lambda b, g, c: (0, 0, g)),
        pl.BlockSpec((1, HALO, N), lambda b, g, c: (0, 0, g)),
        pl.BlockSpec((1, 1, N, pg), lambda b, g, c: (0, g, 0, 0)),
    ]
    out_specs = [
        pl.BlockSpec((R, pg), lambda b, g, c: (row(b, g, c), g)),
        pl.BlockSpec((1, 1, N, pg), lambda b, g, c: (b, g, 0, 0)),
        pl.BlockSpec((1, HALO, pg), lambda b, g, c: (b, 0, g)),
        pl.BlockSpec((1, HALO, N), lambda b, g, c: (b, 0, g)),
        pl.BlockSpec((1, HALO, N), lambda b, g, c: (b, 0, g)),
    ]
    out_shape = [
        jax.ShapeDtypeStruct((n_batch * seq, d_inner), F32),
        jax.ShapeDtypeStruct((n_batch, G, N, pg), F32),
        jax.ShapeDtypeStruct((n_batch, HALO, d_inner), BF16),
        jax.ShapeDtypeStruct((n_batch, HALO, G * N), BF16),
        jax.ShapeDtypeStruct((n_batch, HALO, G * N), BF16),
    ]
    return pl.pallas_call(
        kern,
        grid=(n_batch, G, n_c),
        in_specs=in_specs,
        out_specs=out_specs,
        out_shape=out_shape,
        scratch_shapes=[pltpu.VMEM((R + HALO, pg), BF16), pltpu.VMEM((R + HALO, N), BF16),
                        pltpu.VMEM((R + HALO, N), BF16), pltpu.VMEM((N, pg), F32)],
        compiler_params=_cparams("parallel", "parallel", "arbitrary"),
        name="ssd_scan",
    )(big, big, big, big, dtg, conv_w, conv_w, conv_w, conv_b, conv_b, conv_b,
      head_params, dskip_x, shift, spread, halo_x, halo_b, halo_c, state0)


def _outproj_kernel(attn_ref, y_ref, x_ref, ga_ref, gs_ref, w_ref, gf_ref, wr_ref, br_ref,
                    h1_ref, n2_ref, logit_ref, *, attn_width):
    def norm(v, g):
        return v * lax.rsqrt(jnp.mean(v * v, axis=-1, keepdims=True) + EPS) * g

    an = norm(attn_ref[...].astype(F32), ga_ref[...]).astype(BF16)
    yn = norm(y_ref[...], gs_ref[...]).astype(BF16)
    mixed = (jnp.dot(an, w_ref[0:attn_width, :], preferred_element_type=F32)
             + jnp.dot(yn, w_ref[attn_width:, :], preferred_element_type=F32))
    h1 = x_ref[...] + mixed
    n2 = norm(h1, gf_ref[...])
    h1_ref[...] = h1
    half = n2.shape[1] // 2
    n2_ref[...] = _pack_bf16_pair(n2[:, :half], n2[:, half:])
    logit_ref[...] = jnp.dot(n2.astype(BF16), wr_ref[...], preferred_element_type=F32) + br_ref[...]


def _outproj(attn, yssm, x, g_attn, g_ssm, w_out, g_ffn, w_router, b_router, tm=512):
    m, d = x.shape
    aw, sw = attn.shape[1], yssm.shape[1]
    tm = min(tm, m)
    row = lambda w: pl.BlockSpec((tm, w), lambda i: (i, 0))
    full = lambda r, c: pl.BlockSpec((r, c), lambda i: (0, 0))
    return pl.pallas_call(
        functools.partial(_outproj_kernel, attn_width=aw),
        grid=(m // tm,),
        in_specs=[row(aw), row(sw), row(d), full(1, aw), full(1, sw), full(aw + sw, d), full(1, d),
                  full(d, LANES), full(1, LANES)],
        out_specs=[row(d), row(d // 2), row(LANES)],
        out_shape=[jax.ShapeDtypeStruct((m, d), F32), jax.ShapeDtypeStruct((m, d // 2), U32),
                   jax.ShapeDtypeStruct((m, LANES), F32)],
        compiler_params=_cparams("parallel"),
        name="outproj_router",
    )(attn, yssm, x, g_attn.reshape(1, aw), g_ssm.reshape(1, sw), w_out, g_ffn.reshape(1, d),
      w_router, b_router)


def _route_kernel(logit_ref, idx_ref, gate_ref, cnt_ref, carry_ref, *, tm):
    i = pl.program_id(0)

    @pl.when(i == 0)
    def _():
        carry_ref[...] = jnp.zeros_like(carry_ref)

    work = logit_ref[...]
    lane = lax.broadcasted_iota(I32, work.shape, 1)
    lane_f = lane.astype(F32)
    vals, ids, hots = [], [], []
    for _ in range(TOP_K):
        mk = jnp.max(work, axis=-1, keepdims=True)
        ik = jnp.min(jnp.where(work == mk, lane_f, float(LANES)), axis=-1, keepdims=True)
        hot = lane_f == ik
        work = jnp.where(hot, -jnp.inf, work)
        vals.append(mk)
        ids.append(ik)
        hots.append(hot)
    exps = [jnp.exp(v - vals[0]) for v in vals]
    denom = exps[0] + exps[1] + exps[2] + exps[3]
    sel = jnp.zeros(work.shape, F32)
    for hot in hots:
        sel = sel + hot.astype(F32)
    strict = (lax.broadcasted_iota(I32, (tm, tm), 1) < lax.broadcasted_iota(I32, (tm, tm), 0)).astype(BF16)
    before = jnp.dot(strict, sel.astype(BF16), preferred_element_type=F32) + carry_ref[...]
    idx_out = jnp.zeros(work.shape, I32)
    gate_out = jnp.zeros(work.shape, F32)
    for k in range(TOP_K):
        rank = jnp.sum(jnp.where(hots[k], before, 0.0), axis=-1, keepdims=True).astype(I32)
        idx_out = jnp.where(lane == k, ids[k].astype(I32), idx_out)
        idx_out = jnp.where(lane == TOP_K + k, rank, idx_out)
        gate_out = jnp.where(lane == k, exps[k] / denom, gate_out)
    idx_ref[...] = idx_out
    gate_ref[...] = gate_out
    carry_ref[...] = carry_ref[...] + jnp.sum(sel, axis=0, keepdims=True)
    cnt_ref[...] = carry_ref[...]


def _route(logits, tm=512):
    m = logits.shape[0]
    tm = min(tm, m)
    return pl.pallas_call(
        functools.partial(_route_kernel, tm=tm),
        grid=(m // tm,),
        in_specs=[pl.BlockSpec((tm, LANES), lambda i: (i, 0))],
        out_specs=[pl.BlockSpec((tm, LANES), lambda i: (i, 0)), pl.BlockSpec((tm, LANES), lambda i: (i, 0)),
                   pl.BlockSpec((1, LANES), lambda i: (0, 0))],
        out_shape=[jax.ShapeDtypeStruct((m, LANES), I32), jax.ShapeDtypeStruct((m, LANES), F32),
                   jax.ShapeDtypeStruct((1, LANES), F32)],
        scratch_shapes=[pltpu.VMEM((1, LANES), F32)],
        compiler_params=_cparams("arbitrary"),
        name="route_top4",
    )(logits)


def _dispatch_kernel(dest_ref, src_ref, init_ref, out_ref, sem, *, td):
    del init_ref

    def issue(j, c):
        for k in range(TOP_K):
            pltpu.make_async_copy(src_ref.at[pl.ds(j, 1)], out_ref.at[pl.ds(dest_ref[j * TOP_K + k], 1)],
                                  sem).start(priority=k % 2)
        return c

    lax.fori_loop(0, td, issue, 0, unroll=8)
    for _ in range(TOP_K):
        pltpu.make_async_copy(src_ref, out_ref.at[pl.ds(0, td)], sem).wait()


def _dispatch(dest_flat, src, n_rows_out, td=512):
    m, d = src.shape
    td = min(td, m)
    init = jnp.zeros((n_rows_out, d), src.dtype)
    return pl.pallas_call(
        functools.partial(_dispatch_kernel, td=td),
        grid=(m // td,),
        in_specs=[pl.BlockSpec((td * TOP_K,), lambda i: (i,), memory_space=pltpu.SMEM),
                  pl.BlockSpec((td, d), lambda i: (i, 0)), pl.BlockSpec(memory_space=pl.ANY)],
        out_specs=pl.BlockSpec(memory_space=pl.ANY),
        out_shape=jax.ShapeDtypeStruct((n_rows_out, d), src.dtype),
        scratch_shapes=[pltpu.SemaphoreType.DMA],
        input_output_aliases={2: 0},
        compiler_params=_cparams("arbitrary"),
        name="moe_dispatch",
    )(dest_flat, src, init)


def _grouped_kernel(te_ref, first_ref, nxt_ref, cidx_ref, valid_ref, cnt_ref, x_hbm, w_hbm, b_ref, o_hbm,
                    xbuf, wstage, wbf, obuf, zbuf, x_sem, w_sem, o_sem, z_sem,
                    *, tm, tn, to, n_sweeps, n_pieces, n_tiles, tile_fn, zero_fn):
    n = pl.program_id(0)
    n_used, n_chg = cnt_ref[0], cnt_ref[1]

    def x_copy(m, slot):
        return pltpu.make_async_copy(x_hbm.at[pl.ds(m * tm, tm)], xbuf.at[slot], x_sem.at[slot])

    def w_copy(e, sweep, slot, p):
        col = pl.multiple_of((p * n_sweeps + sweep) * tn, tn)
        return pltpu.make_async_copy(w_hbm.at[e, :, pl.ds(col, tn)], wstage.at[slot, p], w_sem.at[slot])

    def o_copy(m, slot):
        return pltpu.make_async_copy(obuf.at[slot], o_hbm.at[pl.ds(m * tm, tm), pl.ds(n * to, to)], o_sem.at[slot])

    def z_copy(m):
        return pltpu.make_async_copy(zbuf, o_hbm.at[pl.ds(m * tm, tm), pl.ds(n * to, to)], z_sem)

    @pl.when(n == 0)
    def _():
        zbuf[...] = zero_fn(zbuf.shape)
        x_copy(0, 0).start()
        for p in range(n_pieces):
            w_copy(te_ref[0], 0, 0, p).start()

    def tile(m, carry):
        gm = n * n_used + m
        slot = lax.rem(gm, 2)

        @pl.when(first_ref[m] == 1)
        def _():
            ws = lax.rem(n * n_chg + cidx_ref[m], 2)
            for p in range(n_pieces):
                w_copy(0, 0, ws, p).wait()
            for p in range(n_pieces):
                wbf[p] = wstage[ws, p].astype(BF16)
            nm = nxt_ref[m]

            @pl.when(nm >= 0)
            def _():
                for p in range(n_pieces):
                    w_copy(te_ref[jnp.maximum(nm, 0)], n, 1 - ws, p).start()

            @pl.when(jnp.logical_and(nm < 0, n + 1 < n_sweeps))
            def _():
                for p in range(n_pieces):
                    w_copy(te_ref[0], n + 1, 1 - ws, p).start()

        x_copy(0, slot).wait()

        @pl.when(m + 1 < n_used)
        def _():
            x_copy(m + 1, 1 - slot).start()

        @pl.when(jnp.logical_and(m + 1 >= n_used, n + 1 < n_sweeps))
        def _():
            x_copy(0, 1 - slot).start()

        e = te_ref[m]
        biases = [b_ref[pl.ds(e, 1), pl.ds(pl.multiple_of((p * n_sweeps + n) * tn, tn), tn)]
                  for p in range(n_pieces)]

        @pl.when(gm >= 2)
        def _():
            o_copy(0, slot).wait()

        quarter = tm // 4
        n_quarters = jnp.clip((valid_ref[m] + quarter - 1) // quarter, 1, 4)
        for nq in range(1, 5):
            @pl.when(n_quarters == nq)
            def _(rows=nq * quarter):
                obuf[slot, 0:rows] = tile_fn(xbuf[slot, 0:rows], wbf, biases)
                if rows < tm:
                    obuf[slot, rows:tm] = zero_fn((tm - rows, to))

        o_copy(m, slot).start()
        return carry

    lax.fori_loop(0, n_used, tile, 0)

    def zero_tail(m, carry):
        z_copy(m).start()
        return carry

    def zero_tail_wait(m, carry):
        z_copy(0).wait()
        return carry

    lax.fori_loop(n_used, n_tiles, zero_tail, 0)
    lax.fori_loop(n_used, n_tiles, zero_tail_wait, 0)

    @pl.when(n == n_sweeps - 1)
    def _():
        total = n_sweeps * n_used
        for back in (1, 2):
            @pl.when(total >= back)
            def _():
                o_copy(0, lax.rem(total - back, 2)).wait()


def _gate_up_tile(x, wbf, biases):
    lo, hi = _unpack_bf16_pair(x)
    lo, hi = lo.astype(BF16), hi.astype(BF16)
    half = lo.shape[1]

    def proj(p):
        return (jnp.dot(lo, wbf[p, 0:half, :], preferred_element_type=F32)
                + jnp.dot(hi, wbf[p, half:, :], preferred_element_type=F32) + biases[p])

    g = jnp.minimum(proj(0), SWIGLU_LIMIT)
    u = jnp.clip(proj(1), -SWIGLU_LIMIT, SWIGLU_LIMIT)
    return (g * _sigmoid(SWIGLU_ALPHA * g) * (u + 1.0)).astype(BF16)


def _down_tile(h, wbf, biases):
    y = jnp.dot(h, wbf[0], preferred_element_type=F32) + biases[0]
    half = y.shape[1] // 2
    return _pack_bf16_pair(y[:, :half], y[:, half:])


def _packed_zeros(shape):
    zero = jnp.zeros(shape, F32)
    return _pack_bf16_pair(zero, zero)


def _grouped_matmul(plan, x, w, bias, out_dtype, *, tm, tn, to, n_sweeps, n_pieces, tile_fn, zero_fn, name):
    rows, kx = x.shape
    n_tiles = rows // tm
    k = w.shape[1]
    kern = functools.partial(_grouped_kernel, tm=tm, tn=tn, to=to, n_sweeps=n_sweeps, n_pieces=n_pieces,
                             n_tiles=n_tiles, tile_fn=tile_fn, zero_fn=zero_fn)
    any_spec = pl.BlockSpec(memory_space=pl.ANY)
    grid_spec = pltpu.PrefetchScalarGridSpec(
        num_scalar_prefetch=6,
        grid=(n_sweeps,),
        in_specs=[any_spec, any_spec, pl.BlockSpec(bias.shape, lambda n, *_: (0, 0))],
        out_specs=any_spec,
        scratch_shapes=[pltpu.VMEM((2, tm, kx), x.dtype), pltpu.VMEM((2, n_pieces, k, tn), F32),
                        pltpu.VMEM((n_pieces, k, tn), BF16), pltpu.VMEM((2, tm, to), out_dtype),
                        pltpu.VMEM((tm, to), out_dtype), pltpu.SemaphoreType.DMA((2,)),
                        pltpu.SemaphoreType.DMA((2,)), pltpu.SemaphoreType.DMA((2,)), pltpu.SemaphoreType.DMA],
    )
    return pl.pallas_call(
        kern,
        grid_spec=grid_spec,
        out_shape=jax.ShapeDtypeStruct((rows, n_sweeps * to), out_dtype),
        compiler_params=_cparams("arbitrary"),
        name=name,
    )(*plan, x, w, bias)


def _grouped_gate_up(plan, xs, w_gu, b_gu, tm, tn=512):
    ff = w_gu.shape[2] // 2
    tn = min(tn, ff)
    return _grouped_matmul(plan, xs, w_gu, b_gu, BF16, tm=tm, tn=tn, to=tn, n_sweeps=ff // tn, n_pieces=2,
                           tile_fn=_gate_up_tile, zero_fn=lambda s: jnp.zeros(s, BF16), name="moe_gate_up")


def _grouped_down(plan, hdn, w_d, b_d, tm, tn=1024):
    d = w_d.shape[2]
    tn = min(tn, d)
    return _grouped_matmul(plan, hdn, w_d, b_d, U32, tm=tm, tn=tn, to=tn // 2, n_sweeps=d // tn, n_pieces=1,
                           tile_fn=_down_tile, zero_fn=_packed_zeros, name="moe_down")


def _combine_kernel(dest_ref, ys_ref, gate_ref, h1_ref, g_ref, o_ref, buf_ref, sem, *, tc, pack_cols):
    def issue(j, c):
        for k in range(TOP_K):
            pltpu.make_async_copy(ys_ref.at[pl.ds(dest_ref[j * TOP_K + k], 1)], buf_ref.at[k, pl.ds(j, 1)],
                                  sem).start(priority=k % 2)
        return c

    lax.fori_loop(0, tc, issue, 0, unroll=8)
    for k in range(TOP_K):
        pltpu.make_async_copy(ys_ref.at[pl.ds(0, tc)], buf_ref.at[k], sem).wait()

    gates = gate_ref[...]
    hw = pack_cols // 2
    cols = []
    for n in range(buf_ref.shape[2] // hw):
        lo_acc = h1_ref[:, n * pack_cols:n * pack_cols + hw]
        hi_acc = h1_ref[:, n * pack_cols + hw:(n + 1) * pack_cols]
        for k in range(TOP_K):
            lo, hi = _unpack_bf16_pair(buf_ref[k, :, n * hw:(n + 1) * hw])
            lo_acc = lo_acc + gates[:, k:k + 1] * lo
            hi_acc = hi_acc + gates[:, k:k + 1] * hi
        cols += [lo_acc, hi_acc]
    h2 = jnp.concatenate(cols, axis=1)
    o_ref[...] = h2 * lax.rsqrt(jnp.mean(h2 * h2, axis=-1, keepdims=True) + EPS) * g_ref[...]


def _combine(dest_flat, ys, gates, h1, g_final, pack_cols, tc=512):
    m, d = h1.shape
    tc = min(tc, m)
    return pl.pallas_call(
        functools.partial(_combine_kernel, tc=tc, pack_cols=pack_cols),
        grid=(m // tc,),
        in_specs=[pl.BlockSpec((tc * TOP_K,), lambda i: (i,), memory_space=pltpu.SMEM),
                  pl.BlockSpec(memory_space=pl.ANY),
                  pl.BlockSpec((tc, LANES), lambda i: (i, 0)),
                  pl.BlockSpec((tc, d), lambda i: (i, 0)),
                  pl.BlockSpec((1, d), lambda i: (0, 0))],
        out_specs=pl.BlockSpec((tc, d), lambda i: (i, 0)),
        out_shape=jax.ShapeDtypeStruct((m, d), F32),
        scratch_shapes=[pltpu.VMEM((TOP_K, tc, d // 2), U32), pltpu.SemaphoreType.DMA],
        compiler_params=_cparams("arbitrary"),
        name="moe_combine",
    )(dest_flat, ys, gates, h1, g_final.reshape(1, d))


def _lane_pad(v, width=LANES):
    return jnp.pad(v, ((0, 0), (0, width - v.shape[1])))


def kernel(x, meta_tokens, norm_mix_g, w_in, fgate_b, attn_norm_g, conv_w, conv_b, dt_bias, a_log, d_skip,
           ssm_norm_g, w_out, norm_ffn_g, w_router, b_router, w_gate_up, b_gate_up, w_down, b_down,
           final_norm_g):
    n_batch, seq, d = x.shape
    aw = (d // 128) * HEAD_DIM
    n_ah = aw // HEAD_DIM
    sw = d
    n_sh = sw // HEAD_DIM
    hpg = n_sh // SSM_GROUPS
    gn = SSM_GROUPS * SSM_STATE
    tokens = n_batch * seq
    lyr = 0

    o_f = 3 * aw
    o_z = o_f + n_ah
    o_xbc = o_z + sw
    o_dt = o_xbc + sw + 2 * gn
    wt = jnp.swapaxes(w_in[lyr], 0, 1)
    w_q = wt[:aw] * (LOG2E * HEAD_DIM ** -0.5)
    w_big = jnp.concatenate([w_q, wt[aw:o_f], wt[o_z:o_dt]], axis=0).astype(BF16)
    w_dt = jnp.tile(wt[o_dt:o_dt + n_sh].reshape(SSM_GROUPS, hpg, d), (1, HEAD_COPIES, 1))
    w_dt = jnp.pad(w_dt, ((0, 0), (0, LANES - HEAD_COPIES * hpg), (0, 0))).reshape(SSM_GROUPS * LANES, d)
    w_f = jnp.pad(wt[o_f:o_z], ((0, LANES - n_ah), (0, 0)))
    w_small = jnp.concatenate([w_f, w_dt], axis=0).astype(BF16)
    col_k, col_v, col_z, col_x = aw, 2 * aw, 3 * aw, 3 * aw + sw

    meta_rows = jnp.concatenate([jnp.zeros((CHUNK - N_META, d), F32), meta_tokens.astype(F32)], axis=0)
    g_mix = norm_mix_g[lyr]
    n_meta = _rmsnorm_rows(meta_rows, g_mix, CHUNK)
    n_real = _rmsnorm_rows(x.reshape(tokens, d), g_mix, 512)
    big_meta = _matmul(n_meta, w_big, BF16, CHUNK, 1024, "inproj_meta")
    small_meta = _matmul(n_meta, w_small, F32, CHUNK, 640, "inproj_small_meta")
    big = _matmul(n_real, w_big, BF16, 1024, 1024, "inproj")
    small = _matmul(n_real, w_small, F32, 1024, 640, "inproj_small")

    fb = _lane_pad(fgate_b[lyr].reshape(1, n_ah).astype(F32))
    lane_maps = _gate_lane_maps(n_ah)
    xk_meta, _, c_meta_last = _fgate_cumsum(small_meta[:, :LANES], fb, jnp.zeros((1, LANES), F32), lane_maps, 1,
                                            n_lead_pad=CHUNK - N_META)
    xk, yq, _ = _fgate_cumsum(small[:, :LANES], fb, c_meta_last[0], lane_maps, n_batch)
    attn = _attention(big, big_meta, xk, yq, xk_meta, n_batch, seq, n_ah // 2, 0, col_k // LANES,
                      col_v // LANES)

    head_params = jnp.zeros((SSM_GROUPS, 8, LANES), F32)
    per_group = lambda v: jnp.tile(v.reshape(SSM_GROUPS, hpg).astype(F32), (1, HEAD_COPIES))
    head_params = head_params.at[:, 0, :HEAD_COPIES * hpg].set(per_group(dt_bias[lyr]))
    head_params = head_params.at[:, 1, :HEAD_COPIES * hpg].set(per_group(a_log[lyr]))
    dskip_x = jnp.repeat(d_skip[lyr].astype(F32), HEAD_DIM).reshape(1, sw)
    cw, cb = conv_w[lyr].astype(F32), conv_b[lyr].astype(F32).reshape(1, -1)
    dtg_meta, dtg = small_meta[:, LANES:], small[:, LANES:]
    pg = sw // SSM_GROUPS
    zeros_h = lambda w: jnp.zeros((1, HALO, w), BF16)
    state0 = jnp.zeros((1, SSM_GROUPS, SSM_STATE, pg), F32)
    _, s_meta, hx, hb, hc = _ssd(big_meta, dtg_meta, cw, cb, head_params, dskip_x, zeros_h(sw), zeros_h(gn),
                                 zeros_h(gn), state0, 1, CHUNK, sw, col_z, col_x, n_lead_pad=CHUNK - N_META)
    yssm = _ssd(big, dtg, cw, cb, head_params, dskip_x, hx, hb, hc, s_meta, n_batch, seq, sw, col_z, col_x)[0]

    wr = _lane_pad(w_router[lyr]).astype(BF16)
    br = jnp.concatenate([b_router[lyr].astype(F32), jnp.full((LANES - N_EXPERTS,), NEG, F32)]).reshape(1, LANES)
    h1, n2, logits = _outproj(attn, yssm, x.reshape(tokens, d), attn_norm_g[lyr], ssm_norm_g[lyr],
                              w_out[lyr].astype(BF16), norm_ffn_g[lyr], wr, br)

    idx, gates, counts = _route(logits)
    tm = min(512, tokens)
    eid, rank = idx[:, :TOP_K], idx[:, TOP_K:2 * TOP_K]
    cnt = counts[0, :N_EXPERTS].astype(I32)
    padded = ((cnt + tm - 1) // tm) * tm
    ends = jnp.cumsum(padded)
    starts = ends - padded
    dest = (starts[eid] + rank).reshape(-1)
    n_tiles = (tokens * TOP_K + N_EXPERTS * (tm - 1)) // tm
    tile_row0 = jnp.arange(n_tiles, dtype=I32) * tm
    tile_expert = jnp.minimum(jnp.sum((ends[None, :] <= tile_row0[:, None]).astype(I32), axis=1), N_EXPERTS - 1)
    n_used = (ends[-1] // tm).astype(I32)
    tile_id = jnp.arange(n_tiles, dtype=I32)
    first = (jnp.concatenate([jnp.ones((1,), bool), tile_expert[1:] != tile_expert[:-1]])
             & (tile_id < n_used)).astype(I32)
    cidx = jnp.cumsum(first) - 1
    later_first = (tile_id[None, :] > tile_id[:, None]) & (first[None, :] == 1)
    nxt = jnp.min(jnp.where(later_first, tile_id[None, :], n_tiles), axis=1)
    nxt = jnp.where(nxt >= n_tiles, -1, nxt).astype(I32)
    valid = jnp.clip(cnt[tile_expert] - (tile_id * tm - starts[tile_expert]), 0, tm)
    valid = jnp.where(tile_id < n_used, valid, 0).astype(I32)
    plan = (tile_expert, first, nxt, cidx.astype(I32), valid, jnp.stack([n_used, jnp.sum(first)]).astype(I32))

    xs_sorted = _dispatch(dest, n2, n_tiles * tm)
    hdn = _grouped_gate_up(plan, xs_sorted, w_gate_up[lyr], b_gate_up[lyr].astype(F32), tm)
    tn_down = min(1024, d)
    ys = _grouped_down(plan, hdn, w_down[lyr], b_down[lyr].astype(F32), tm, tn=tn_down)
    out = _combine(dest, ys, gates, h1, final_norm_g, tn_down)
    return out.reshape(n_batch, seq, d)
```
